```python
import math
import jax, jax.numpy as jnp
from jax import lax
import numpy as np

D_MODEL = 1024
BATCH = 16
SEQ = 2048
DEPTH = 2

CHUNK = 64
Q_BLOCK = 128

D_MIX = D_MODEL
A_HEADS = 4
A_HEAD_DIM = 64
A_WIDTH = A_HEADS * A_HEAD_DIM
DECAY_LORA = 64
ICLR_LORA = 64
GATE_LORA = 128
A_COLS = 3 * A_WIDTH + DECAY_LORA + ICLR_LORA + GATE_LORA
B_HEADS = 4
B_QK_DIM = 64
B_V_DIM = 2 * B_QK_DIM
B_WIDTH = B_HEADS * B_V_DIM
B_QK_COLS = B_HEADS * 2 * B_QK_DIM
B_COLS = 2 * B_QK_COLS + B_WIDTH
C_WIDTH = D_MIX - A_WIDTH - B_WIDTH
CONV_WIDTH = 31
C_COLS = 2 * C_WIDTH
N_IN = A_COLS + B_COLS + C_COLS
D_FF = 4 * D_MODEL
RMS_EPS = 1e-6
LN_EPS = 1e-5
LNX_EPS = 64e-5
NEG_INF = -1e30

kernel_name = 'hybrid_rwkv7_diffattn_conformer_block'


def rms_norm(x, g, eps=RMS_EPS):
    xf = x.astype(jnp.float32)
    y = xf * lax.rsqrt(jnp.mean(xf * xf, axis=-1, keepdims=True) + eps)
    return (y * g.astype(jnp.float32)).astype(x.dtype)


def layer_norm(x, w, b, eps):
    xf = x.astype(jnp.float32)
    mu = jnp.mean(xf, axis=-1, keepdims=True)
    var = jnp.mean(jnp.square(xf - mu), axis=-1, keepdims=True)
    y = (xf - mu) * lax.rsqrt(var + eps)
    return (y * w.astype(jnp.float32) + b.astype(jnp.float32)).astype(x.dtype)


def alibi_slopes(n_heads):
    return jnp.asarray([2.0 ** (-8.0 * (h + 1) / n_heads) for h in range(n_heads)], dtype=jnp.float32)


def rwkv7_mix(p_a, mu, w0, w2, a0, a2, g2, k_k, k_a, r_k, lnx_w, lnx_b):
    bsz, t_len, _ = p_a.shape
    prev = jnp.pad(p_a[:, :-1], ((0, 0), (1, 0), (0, 0)))
    p_a = p_a + mu * (prev - p_a)
    r, k, v, dw, da, dg = jnp.split(
        p_a, [A_WIDTH, 2 * A_WIDTH, 3 * A_WIDTH, 3 * A_WIDTH + DECAY_LORA,
              3 * A_WIDTH + DECAY_LORA + ICLR_LORA], axis=-1)
    w = -jax.nn.softplus(-(w0 + jnp.tanh(dw) @ w2)) - 0.5
    decay = jnp.exp(-jnp.exp(w.astype(jnp.float32)))
    a = jax.nn.sigmoid(a0 + da @ a2)
    g = jax.nn.sigmoid(dg) @ g2
    hs = (bsz, t_len, A_HEADS, A_HEAD_DIM)
    kk = (k * k_k).astype(jnp.float32).reshape(hs)
    kk = kk / jnp.maximum(jnp.linalg.norm(kk, axis=-1, keepdims=True), 1e-12)
    k = k * (1.0 + (a - 1.0) * k_a)
    rh = r.astype(jnp.float32).reshape(hs)
    kh = k.astype(jnp.float32).reshape(hs)
    vh = v.astype(jnp.float32).reshape(hs)
    ah = a.astype(jnp.float32).reshape(hs)
    wh = decay.reshape(hs)

    def step(state, inp):
        r_t, w_t, k_t, v_t, kk_t, a_t = inp
        sa = jnp.einsum('bhij,bhj->bhi', state, -kk_t)
        state = (state * w_t[:, :, None, :]
                 + sa[..., None] * (kk_t * a_t)[:, :, None, :]
                 + v_t[..., None] * k_t[:, :, None, :])
        y_t = jnp.einsum('bhij,bhj->bhi', state, r_t)
        return state, y_t

    xs = tuple(jnp.moveaxis(arr, 1, 0) for arr in (rh, wh, kh, vh, kk, ah))
    s0 = jnp.zeros((bsz, A_HEADS, A_HEAD_DIM, A_HEAD_DIM), jnp.float32)
    _, ys = lax.scan(step, s0, xs)
    y = jnp.moveaxis(ys, 0, 1)
    y = layer_norm(y, lnx_w.reshape(A_HEADS, A_HEAD_DIM), lnx_b.reshape(A_HEADS, A_HEAD_DIM), LNX_EPS)
    bonus = jnp.sum(rh * kh * r_k.astype(jnp.float32), axis=-1, keepdims=True) * vh
    out = (y + bonus).reshape(bsz, t_len, A_WIDTH).astype(p_a.dtype) * g
    return out


def diff_attention(p_b, q_norm_g, k_norm_g, lq1, lk1, lq2, lk2, subln_g, lambda_init):
    bsz, t_len, _ = p_b.shape
    q, k, v = jnp.split(p_b, [B_QK_COLS, 2 * B_QK_COLS], axis=-1)
    q = q.reshape(bsz, t_len, B_HEADS, 2, B_QK_DIM)
    k = k.reshape(bsz, t_len, B_HEADS, 2, B_QK_DIM)
    v = v.reshape(bsz, t_len, B_HEADS, B_V_DIM).astype(jnp.float32)
    q = rms_norm(q, q_norm_g) * (B_QK_DIM ** -0.5)
    k = rms_norm(k, k_norm_g)
    lam = (jnp.exp(jnp.sum(lq1.astype(jnp.float32) * lk1.astype(jnp.float32)))
           - jnp.exp(jnp.sum(lq2.astype(jnp.float32) * lk2.astype(jnp.float32))) + lambda_init)
    slopes = alibi_slopes(B_HEADS)[:, None, None, None]
    outs = []
    for blk in range(t_len // Q_BLOCK):
        q0 = blk * Q_BLOCK
        k_end = q0 + Q_BLOCK
        tq = jnp.arange(q0, k_end)
        tk = jnp.arange(k_end)
        s = jnp.einsum('bqhcd,bkhcd->bhcqk', q[:, q0:k_end], k[:, :k_end]).astype(jnp.float32)
        dist = jnp.abs(tq[:, None] - tk[None, :]).astype(jnp.float32)
        s = s - slopes * dist
        allowed = (tk[None, :] // CHUNK) <= (tq[:, None] // CHUNK)
        s = jnp.where(allowed, s, NEG_INF)
        p = jax.nn.softmax(s, axis=-1)
        attn = p[:, :, 0] - lam * p[:, :, 1]
        outs.append(jnp.einsum('bhqk,bkhe->bqhe', attn, v[:, :k_end]))
    o = jnp.concatenate(outs, axis=1)
    o = rms_norm(o, subln_g) * (1.0 - lambda_init)
    return o.reshape(bsz, t_len, B_WIDTH).astype(p_b.dtype)


def conformer_conv(p_c, conv_w, conv_b, ln_w, ln_b):
    a, b = jnp.split(p_c, 2, axis=-1)
    h = a * jax.nn.sigmoid(b)
    h = lax.conv_general_dilated(
        h, conv_w[:, None, :], window_strides=(1,), padding=[(CONV_WIDTH - 1, 0)],
        dimension_numbers=('NWC', 'WIO', 'NWC'), feature_group_count=C_WIDTH) + conv_b
    h = layer_norm(h, ln_w, ln_b, LN_EPS)
    return jax.nn.silu(h)


def setup_inputs(seed: int = 0) -> dict:
    key = jax.random.key(seed)
    ks = iter(jax.random.split(key, 40))

    def nrm(shape, scale):
        return jax.random.normal(next(ks), shape, jnp.float32) * scale

    def near_one(shape, noise=0.05):
        return 1.0 + nrm(shape, noise)

    L = DEPTH
    return {
        'x': nrm((BATCH, SEQ, D_MODEL), 1.0),
        'c': nrm((BATCH, D_MODEL), 1.0),
        'ada_w': nrm((L, D_MODEL, 6 * D_MODEL), 0.5 * D_MODEL ** -0.5),
        'ada_b': nrm((L, 6 * D_MODEL), 0.01),
        'norm1_g': near_one((L, D_MODEL)),
        'w_in': nrm((L, D_MODEL, N_IN), D_MODEL ** -0.5),
        'tshift_mu': jax.random.uniform(next(ks), (L, A_COLS), jnp.float32),
        'decay_w0': jax.random.uniform(next(ks), (L, A_WIDTH), jnp.float32, minval=-6.0, maxval=-1.0),
        'decay_w2': nrm((L, DECAY_LORA, A_WIDTH), 0.1 * DECAY_LORA ** -0.5),
        'iclr_a0': nrm((L, A_WIDTH), 0.1),
        'iclr_a2': nrm((L, ICLR_LORA, A_WIDTH), 0.5 * ICLR_LORA ** -0.5),
        'gate_g2': nrm((L, GATE_LORA, A_WIDTH), GATE_LORA ** -0.5),
        'k_k': 0.85 + nrm((L, A_WIDTH), 0.05),
        'k_a': near_one((L, A_WIDTH)),
        'r_k': nrm((L, A_HEADS, A_HEAD_DIM), 0.1),
        'lnx_w': near_one((L, A_WIDTH)),
        'lnx_b': nrm((L, A_WIDTH), 0.01),
        'q_norm_g': near_one((L, B_QK_DIM)),
        'k_norm_g': near_one((L, B_QK_DIM)),
        'lambda_q1': nrm((L, B_QK_DIM), 0.1),
        'lambda_k1': nrm((L, B_QK_DIM), 0.1),
        'lambda_q2': nrm((L, B_QK_DIM), 0.1),
        'lambda_k2': nrm((L, B_QK_DIM), 0.1),
        'subln_g': near_one((L, B_V_DIM)),
        'conv_w': nrm((L, CONV_WIDTH, C_WIDTH), CONV_WIDTH ** -0.5),
        'conv_b': nrm((L, C_WIDTH), 0.01),
        'conv_ln_w': near_one((L, C_WIDTH)),
        'conv_ln_b': nrm((L, C_WIDTH), 0.01),
        'w_out': nrm((L, D_MIX, D_MODEL), D_MIX ** -0.5),
        'norm2_g': near_one((L, D_MODEL)),
        'mlp_w1': nrm((L, D_MODEL, D_FF), D_MODEL ** -0.5),
        'mlp_w2': nrm((L, D_FF, D_MODEL), D_FF ** -0.5),
    }


def reference(x, c, ada_w, ada_b, norm1_g, w_in, tshift_mu, decay_w0, decay_w2, iclr_a0,
              iclr_a2, gate_g2, k_k, k_a, r_k, lnx_w, lnx_b, q_norm_g, k_norm_g,
              lambda_q1, lambda_k1, lambda_q2, lambda_k2, subln_g, conv_w, conv_b,
              conv_ln_w, conv_ln_b, w_out, norm2_g, mlp_w1, mlp_w2):
    cond = jax.nn.silu(c)
    for i in range(DEPTH):
        mod = cond @ ada_w[i] + ada_b[i]
        sh1, sc1, g1, sh2, sc2, g2 = [m[:, None, :] for m in jnp.split(mod, 6, axis=-1)]
        lambda_init = 0.8 - 0.6 * math.exp(-0.3 * i)

        h = rms_norm(x, norm1_g[i]) * (1.0 + sc1) + sh1
        proj = h @ w_in[i]
        p_a, p_b, p_c = jnp.split(proj, [A_COLS, A_COLS + B_COLS], axis=-1)
        y_a = rwkv7_mix(p_a, tshift_mu[i], decay_w0[i], decay_w2[i], iclr_a0[i], iclr_a2[i],
                        gate_g2[i], k_k[i], k_a[i], r_k[i], lnx_w[i], lnx_b[i])
        y_b = diff_attention(p_b, q_norm_g[i], k_norm_g[i], lambda_q1[i], lambda_k1[i],
                             lambda_q2[i], lambda_k2[i], subln_g[i], lambda_init)
        y_c = conformer_conv(p_c, conv_w[i], conv_b[i], conv_ln_w[i], conv_ln_b[i])
        y = jnp.concatenate([y_a, y_b, y_c], axis=-1) @ w_out[i]
        x = x + g1 * y

        h = rms_norm(x, norm2_g[i]) * (1.0 + sc2) + sh2
        f = jnp.square(jax.nn.relu(h @ mlp_w1[i])) @ mlp_w2[i]
        x = x + g2 * f
    return x
```

```python
import functools
import math

import jax
import jax.numpy as jnp
from jax import lax
from jax.experimental import pallas as pl
from jax.experimental.pallas import tpu as pltpu

F32 = jnp.float32
BF16 = jnp.bfloat16

D_MODEL = 1024
DEPTH = 2
CHUNK = 64
A_HEADS = 4
A_HEAD_DIM = 64
A_WIDTH = A_HEADS * A_HEAD_DIM
DECAY_LORA = 64
ICLR_LORA = 64
GATE_LORA = 128
A_COLS = 3 * A_WIDTH + DECAY_LORA + ICLR_LORA + GATE_LORA
B_HEADS = 4
B_QK_DIM = 64
B_V_DIM = 2 * B_QK_DIM
B_WIDTH = B_HEADS * B_V_DIM
B_QK_COLS = B_HEADS * 2 * B_QK_DIM
B_COLS = 2 * B_QK_COLS + B_WIDTH
C_WIDTH = D_MODEL - A_WIDTH - B_WIDTH
CONV_WIDTH = 31
C_COLS = 2 * C_WIDTH
N_IN = A_COLS + B_COLS + C_COLS
D_FF = 4 * D_MODEL
RMS_EPS = 1e-6
LN_EPS = 1e-5
LNX_EPS = 64e-5
NEG_INF = -1e30
EXP_NEG_HALF = math.exp(-0.5)

V7X_LANES = 128
V7X_SUBLANES = 8
V7X_VMEM_LIMIT_BYTES = 56 * 1024 * 1024

RWKV_CHUNK = 64
RWKV_TILE = 256
ATTN_TQ = 128
ATTN_TK = 128
CONV_TILE = 256
CONV_HALO = 32
PROJ_TM = 512
MLP_TM = 1024
MLP_TF = 1024


def _params(*sem):
    return pltpu.CompilerParams(dimension_semantics=sem,
                                vmem_limit_bytes=V7X_VMEM_LIMIT_BYTES)


def _dot(a, b):
    return jnp.dot(a.astype(BF16), b.astype(BF16), preferred_element_type=F32)


def _dot_nt(a, b):
    return lax.dot_general(a.astype(BF16), b.astype(BF16), (((1,), (1,)), ((), ())),
                           preferred_element_type=F32)


def _dot_tn(a, b):
    return lax.dot_general(a.astype(BF16), b.astype(BF16), (((0,), (0,)), ((), ())),
                           preferred_element_type=F32)


def _split_dot(a, b_exact, parts):
    out = None
    rem = a
    for _ in range(parts):
        piece = rem.astype(BF16)
        term = jnp.dot(piece, b_exact, preferred_element_type=F32)
        out = term if out is None else out + term
        rem = rem - piece.astype(F32)
    return out


def _sigmoid(x):
    return 1.0 / (1.0 + jnp.exp(-x))


def _ada_kernel(c_ref, w_ref, b_ref, o_ref):
    c = c_ref[...]
    cond = c * _sigmoid(c)
    o_ref[0] = _dot(cond, w_ref[0]) + b_ref[0]


def _ada_modulation(c, ada_w, ada_b):
    depth, d, n = ada_w.shape
    bsz = c.shape[0]
    tn = 1536
    return pl.pallas_call(
        _ada_kernel,
        grid=(depth, n // tn),
        in_specs=[
            pl.BlockSpec((bsz, d), lambda l, j: (0, 0)),
            pl.BlockSpec((1, d, tn), lambda l, j: (l, 0, j)),
            pl.BlockSpec((1, 1, tn), lambda l, j: (l, 0, j)),
        ],
        out_specs=pl.BlockSpec((1, bsz, tn), lambda l, j: (l, 0, j)),
        out_shape=jax.ShapeDtypeStruct((depth, bsz, n), F32),
        compiler_params=_params("arbitrary", "arbitrary"),
        name="ada_modulation",
    )(c, ada_w, ada_b.reshape(depth, 1, n))


def _norm_mod(x, g, sc, sh):
    ms = jnp.mean(x * x, axis=-1, keepdims=True)
    y = x * lax.rsqrt(ms + RMS_EPS) * g
    return y * (1.0 + sc) + sh


def _inproj_kernel(x_ref, g_ref, sc_ref, sh_ref, w_ref, o_ref, *, tn):
    h = _norm_mod(x_ref[0], g_ref[...], sc_ref[0], sh_ref[0]).astype(BF16)
    n = w_ref.shape[1]
    for j in range(n // tn):
        o_ref[0, :, j * tn:(j + 1) * tn] = jnp.dot(
            h, w_ref[:, j * tn:(j + 1) * tn], preferred_element_type=F32)


def _in_projection(x, g, sc, sh, w, tm):
    bsz, t, d = x.shape
    n = w.shape[1]
    return pl.pallas_call(
        functools.partial(_inproj_kernel, tn=1024),
        grid=(bsz, t // tm),
        in_specs=[
            pl.BlockSpec((1, tm, d), lambda b, i: (b, i, 0)),
            pl.BlockSpec((1, d), lambda b, i: (0, 0)),
            pl.BlockSpec((1, 1, d), lambda b, i: (b, 0, 0)),
            pl.BlockSpec((1, 1, d), lambda b, i: (b, 0, 0)),
            pl.BlockSpec((d, n), lambda b, i: (0, 0)),
        ],
        out_specs=pl.BlockSpec((1, tm, n), lambda b, i: (b, i, 0)),
        out_shape=jax.ShapeDtypeStruct((bsz, t, n), F32),
        compiler_params=_params("arbitrary", "arbitrary"),
        name="in_projection",
    )(x, g, sc, sh, w)


def _rwkv_kernel(pa_ref, mu_ref, w0_ref, w2_ref, a0_ref, a2_ref, g2_ref, kk_ref, ka_ref,
                 rk_ref, lw_ref, lb_ref, o_ref, carry_ref, s_ref, *, tile, chunk):
    aw = A_WIDTH
    hd = A_HEAD_DIM

    @pl.when(pl.program_id(1) == 0)
    def _():
        carry_ref[...] = jnp.zeros_like(carry_ref)
        s_ref[...] = jnp.zeros_like(s_ref)

    pa = pa_ref[0]
    row = lax.broadcasted_iota(jnp.int32, (tile, 1), 0)
    prev = jnp.where(row == 0, carry_ref[0:1, :], pltpu.roll(pa, 1, 0))
    carry_ref[0:1, :] = pa[tile - 1:tile, :]
    xs = pa + mu_ref[...] * (prev - pa)

    r = xs[:, 0:aw]
    k = xs[:, aw:2 * aw]
    v = xs[:, 2 * aw:3 * aw]
    lora = xs[:, 3 * aw:3 * aw + DECAY_LORA + ICLR_LORA]
    dg = xs[:, 3 * aw + DECAY_LORA + ICLR_LORA:]

    u = w0_ref[...] + _dot(jnp.tanh(lora), w2_ref[...])
    logdecay = -EXP_NEG_HALF * _sigmoid(u)
    iclr = _sigmoid(a0_ref[...] + _dot(lora, a2_ref[...]))
    gate = _dot(_sigmoid(dg), g2_ref[...])

    ri = lax.broadcasted_iota(jnp.int32, (aw, aw), 0)
    ci = lax.broadcasted_iota(jnp.int32, (aw, aw), 1)
    same_head = (ri // hd) == (ci // hd)
    head_ones = same_head.astype(F32).astype(BF16)

    def head_sum(x):
        return _split_dot(x, head_ones, 2)

    kk = k * kk_ref[...]
    kk = kk / jnp.maximum(jnp.sqrt(head_sum(kk * kk)), 1e-12)
    k = k * (1.0 + (iclr - 1.0) * ka_ref[...])
    b = kk * iclr

    ti = lax.broadcasted_iota(jnp.int32, (tile, tile), 0)
    si = lax.broadcasted_iota(jnp.int32, (tile, tile), 1)
    tri = ((si <= ti) & ((si // chunk) == (ti // chunk))).astype(F32).astype(BF16)
    cum = None
    rem = logdecay
    for _ in range(3):
        piece = rem.astype(BF16)
        term = jnp.dot(tri, piece, preferred_element_type=F32)
        cum = term if cum is None else cum + term
        rem = rem - piece.astype(F32)

    t_c = lax.broadcasted_iota(jnp.int32, (chunk, aw), 0)
    s_c = lax.broadcasted_iota(jnp.int32, (chunk, aw), 1) % chunk
    strict = s_c < t_c
    incl = s_c <= t_c
    eye_c = (s_c == t_c).astype(F32)
    eye_full = ri == ci
    lane_head = lax.broadcasted_iota(jnp.int32, (hd, aw), 1) // hd

    def blockdiag(x):
        return jnp.where(same_head, jnp.concatenate([x] * A_HEADS, axis=0), 0.0)

    def per_head(p, q):
        return _dot(p, blockdiag(q))

    state = s_ref[...]
    ys = []
    for c in range(tile // chunk):
        sl = slice(c * chunk, (c + 1) * chunk)
        cum_c = cum[sl]
        ld_c = logdecay[sl]
        total = cum_c[chunk - 1:chunk, :]
        w_in = jnp.exp(cum_c)
        w_ex = jnp.exp(cum_c - ld_c)
        w_inv = jnp.exp(-cum_c)
        w_end = jnp.exp(total - cum_c)
        rt = r[sl] * w_in
        at = -kk[sl] * w_ex
        bt = b[sl] * w_inv
        kt = k[sl] * w_inv
        be = b[sl] * w_end
        ke = k[sl] * w_end
        vc = v[sl]

        lhs = jnp.concatenate([at, rt], axis=0)
        ab = _dot_nt(lhs, blockdiag(bt))
        ak = _dot_nt(lhs, blockdiag(kt))
        low = jnp.where(strict, ab[:chunk], 0.0)
        a_ak = jnp.where(strict, ak[:chunk], 0.0)
        a_rb = jnp.where(incl, ab[chunk:], 0.0)
        a_rk = jnp.where(incl, ak[chunk:], 0.0)

        tinv = eye_c + low
        pw = low
        for _ in range(int(math.log2(chunk)) - 1):
            pw = per_head(pw, pw)
            tinv = tinv + per_head(tinv, pw)

        ah = per_head(tinv, at)
        vh = per_head(tinv, per_head(a_ak, vc))
        rh = rt + per_head(a_rb, ah)
        y_in = per_head(a_rb, vh) + per_head(a_rk, vc)

        trans = jnp.where(same_head, _dot_tn(ah, be), 0.0)
        trans = trans + jnp.where(eye_full, jnp.exp(total), 0.0)
        add_full = _dot_tn(jnp.concatenate([vh, vc], axis=0),
                           jnp.concatenate([be, ke], axis=0))
        add = None
        for h in range(A_HEADS):
            blk = jnp.where(lane_head == h, add_full[h * hd:(h + 1) * hd], 0.0)
            add = blk if add is None else add + blk

        ys.append(_dot_nt(rh, blockdiag(state)) + y_in)
        state = _dot(state, trans) + add

    s_ref[...] = state
    y = jnp.concatenate(ys, axis=0)

    inv_n = 1.0 / hd
    mu_h = head_sum(y) * inv_n
    yc = y - mu_h
    var_h = head_sum(yc * yc) * inv_n
    yn = yc * lax.rsqrt(var_h + LNX_EPS) * lw_ref[...] + lb_ref[...]
    bonus = head_sum(r * k * rk_ref[...]) * v
    o_ref[0] = ((yn + bonus) * gate).astype(o_ref.dtype)


def _rwkv_mix(proj, mu, w0, w2p, a0, a2p, g2, k_k, k_a, r_k, lnx_w, lnx_b, tile):
    bsz, t, _ = proj.shape
    vec = lambda n: pl.BlockSpec((1, n), lambda b, i: (0, 0))
    mat = lambda m, n: pl.BlockSpec((m, n), lambda b, i: (0, 0))
    return pl.pallas_call(
        functools.partial(_rwkv_kernel, tile=tile, chunk=RWKV_CHUNK),
        grid=(bsz, t // tile),
        in_specs=[
            pl.BlockSpec((1, tile, A_COLS), lambda b, i: (b, i, 0)),
            vec(A_COLS), vec(A_WIDTH), mat(DECAY_LORA + ICLR_LORA, A_WIDTH),
            vec(A_WIDTH), mat(DECAY_LORA + ICLR_LORA, A_WIDTH), mat(GATE_LORA, A_WIDTH),
            vec(A_WIDTH), vec(A_WIDTH), vec(A_WIDTH), vec(A_WIDTH), vec(A_WIDTH),
        ],
        out_specs=pl.BlockSpec((1, tile, A_WIDTH), lambda b, i: (b, i, 0)),
        out_shape=jax.ShapeDtypeStruct((bsz, t, A_WIDTH), BF16),
        scratch_shapes=[pltpu.VMEM((V7X_SUBLANES, A_COLS), F32),
                        pltpu.VMEM((A_HEAD_DIM, A_WIDTH), F32)],
        compiler_params=_params("arbitrary", "arbitrary"),
        name="rwkv7_mix",
    )(proj, mu, w0, w2p, a0, a2p, g2, k_k, k_a, r_k, lnx_w, lnx_b)


def _half_rms(x, gain, scale):
    lane = lax.broadcasted_iota(jnp.int32, (1, x.shape[1]), 1)
    lo = lane < B_QK_DIM
    x2 = x * x
    s_lo = jnp.sum(jnp.where(lo, x2, 0.0), axis=-1, keepdims=True)
    s_hi = jnp.sum(jnp.where(lo, 0.0, x2), axis=-1, keepdims=True)
    ms = jnp.where(lo, s_lo, s_hi) * (1.0 / B_QK_DIM)
    return x * lax.rsqrt(ms + RMS_EPS) * (gain * scale)


def _attn_kernel(q_ref, k_ref, v_ref, qg_ref, kg_ref, sg_ref, lq1_ref, lk1_ref, lq2_ref,
                 lk2_ref, slope_ref, o_ref, kn_ref, vb_ref, *, tq, tk, lambda_init):
    i = pl.program_id(2)

    @pl.when(i == 0)
    def _():
        kn_ref[...] = _half_rms(k_ref[0], kg_ref[...], 1.0).astype(BF16)
        vb_ref[...] = v_ref[0].astype(BF16)

    qn = _half_rms(q_ref[0], qg_ref[...], B_QK_DIM ** -0.5)
    lane = lax.broadcasted_iota(jnp.int32, (1, qn.shape[1]), 1)
    lo = lane < B_QK_DIM
    qs = jnp.concatenate([jnp.where(lo, qn, 0.0), jnp.where(lo, 0.0, qn)],
                         axis=0).astype(BF16)
    slope = slope_ref[0]
    slope = slope[:, 0:1]

    rows = lax.broadcasted_iota(jnp.int32, (2 * tq, tk), 0)
    t_q = i * tq + jnp.where(rows >= tq, rows - tq, rows)
    cols = lax.broadcasted_iota(jnp.int32, (2 * tq, tk), 1)

    def body(j, carry):
        m, l, acc = carry
        start = pl.multiple_of(j * tk, tk)
        kb = kn_ref[pl.ds(start, tk), :]
        vb = vb_ref[pl.ds(start, tk), :]
        s = lax.dot_general(qs, kb, (((1,), (1,)), ((), ())), preferred_element_type=F32)
        t_k = j * tk + cols
        s = s - slope * jnp.abs(t_q - t_k).astype(F32)
        s = jnp.where((t_k // CHUNK) <= (t_q // CHUNK), s, NEG_INF)
        m_new = jnp.maximum(m, jnp.max(s, axis=-1, keepdims=True))
        alpha = jnp.exp(m - m_new)
        p = jnp.exp(s - m_new)
        l = alpha * l + jnp.sum(p, axis=-1, keepdims=True)
        acc = alpha * acc + jnp.dot(p.astype(BF16), vb, preferred_element_type=F32)
        return m_new, l, acc

    init = (jnp.full((2 * tq, 1), NEG_INF, F32), jnp.zeros((2 * tq, 1), F32),
            jnp.zeros((2 * tq, B_V_DIM), F32))
    n_blocks = (i * tq + tq + tk - 1) // tk
    _, l, acc = lax.fori_loop(0, n_blocks, body, init)
    o = acc / l
    lam = (jnp.exp(jnp.sum(lq1_ref[...] * lk1_ref[...], axis=-1, keepdims=True))
           - jnp.exp(jnp.sum(lq2_ref[...] * lk2_ref[...], axis=-1, keepdims=True))
           + lambda_init)
    d = o[:tq] - lam * o[tq:]
    ms = jnp.mean(d * d, axis=-1, keepdims=True)
    out = d * lax.rsqrt(ms + RMS_EPS) * sg_ref[...] * (1.0 - lambda_init)
    o_ref[0] = out.astype(o_ref.dtype)


def _diff_attention(proj, qg2, kg2, sg, lq1, lk1, lq2, lk2, slopes, lambda_init):
    bsz, t, _ = proj.shape
    q_blk = A_COLS // B_V_DIM
    k_blk = (A_COLS + B_QK_COLS) // B_V_DIM
    v_blk = (A_COLS + 2 * B_QK_COLS) // B_V_DIM
    vec = lambda n: pl.BlockSpec((1, n), lambda b, h, i: (0, 0))
    return pl.pallas_call(
        functools.partial(_attn_kernel, tq=ATTN_TQ, tk=ATTN_TK, lambda_init=lambda_init),
        grid=(bsz, B_HEADS, t // ATTN_TQ),
        in_specs=[
            pl.BlockSpec((1, ATTN_TQ, B_V_DIM), lambda b, h, i: (b, i, q_blk + h)),
            pl.BlockSpec((1, t, B_V_DIM), lambda b, h, i: (b, 0, k_blk + h)),
            pl.BlockSpec((1, t, B_V_DIM), lambda b, h, i: (b, 0, v_blk + h)),
            vec(B_V_DIM), vec(B_V_DIM), vec(B_V_DIM),
            vec(B_QK_DIM), vec(B_QK_DIM), vec(B_QK_DIM), vec(B_QK_DIM),
            pl.BlockSpec((1, 1, V7X_LANES), lambda b, h, i: (h, 0, 0)),
        ],
        out_specs=pl.BlockSpec((1, ATTN_TQ, B_V_DIM), lambda b, h, i: (b, i, h)),
        out_shape=jax.ShapeDtypeStruct((bsz, t, B_WIDTH), BF16),
        scratch_shapes=[pltpu.VMEM((t, B_V_DIM), BF16), pltpu.VMEM((t, B_V_DIM), BF16)],
        compiler_params=_params("arbitrary", "arbitrary", "arbitrary"),
        name="diff_attention",
    )(proj, proj, proj, qg2, kg2, sg, lq1, lk1, lq2, lk2, slopes)


def _conv_kernel(pc_ref, w_ref, b_ref, lw_ref, lb_ref, o_ref, hbuf_ref, *, tile, rows):
    @pl.when(pl.program_id(1) == 0)
    def _():
        hbuf_ref[0:CONV_HALO, :] = jnp.zeros((CONV_HALO, C_WIDTH), F32)

    pc = pc_ref[0]
    hbuf_ref[CONV_HALO:CONV_HALO + tile, :] = pc[:, :C_WIDTH] * _sigmoid(pc[:, C_WIDTH:])
    lead = CONV_HALO - (CONV_WIDTH - 1)
    for c in range(tile // rows):
        acc = jnp.zeros((rows, C_WIDTH), F32)
        for j in range(CONV_WIDTH):
            start = c * rows + lead + j
            acc = acc + w_ref[j:j + 1, :] * hbuf_ref[start:start + rows, :]
        acc = acc + b_ref[...]
        mu = jnp.mean(acc, axis=-1, keepdims=True)
        d = acc - mu
        var = jnp.mean(d * d, axis=-1, keepdims=True)
        y = d * lax.rsqrt(var + LN_EPS) * lw_ref[...] + lb_ref[...]
        o_ref[0, c * rows:(c + 1) * rows, :] = (y * _sigmoid(y)).astype(o_ref.dtype)
    hbuf_ref[0:CONV_HALO, :] = hbuf_ref[tile:tile + CONV_HALO, :]


def _conformer_conv(proj, conv_w, conv_b, ln_w, ln_b, tile):
    bsz, t, _ = proj.shape
    c_blk = (A_COLS + B_COLS) // C_COLS
    vec = lambda n: pl.BlockSpec((1, n), lambda b, i: (0, 0))
    return pl.pallas_call(
        functools.partial(_conv_kernel, tile=tile, rows=64),
        grid=(bsz, t // tile),
        in_specs=[
            pl.BlockSpec((1, tile, C_COLS), lambda b, i: (b, i, c_blk)),
            pl.BlockSpec((CONV_WIDTH, C_WIDTH), lambda b, i: (0, 0)),
            vec(C_WIDTH), vec(C_WIDTH), vec(C_WIDTH),
        ],
        out_specs=pl.BlockSpec((1, tile, C_WIDTH), lambda b, i: (b, i, 0)),
        out_shape=jax.ShapeDtypeStruct((bsz, t, C_WIDTH), BF16),
        scratch_shapes=[pltpu.VMEM((CONV_HALO + tile, C_WIDTH), F32)],
        compiler_params=_params("arbitrary", "arbitrary"),
        name="conformer_conv",
    )(proj, conv_w, conv_b, ln_w, ln_b)


def _outproj_kernel(x_ref, ya_ref, yb_ref, yc_ref, w_ref, g_ref, o_ref):
    y = jnp.dot(ya_ref[0], w_ref[0:A_WIDTH, :], preferred_element_type=F32)
    y = y + jnp.dot(yb_ref[0], w_ref[A_WIDTH:A_WIDTH + B_WIDTH, :], preferred_element_type=F32)
    y = y + jnp.dot(yc_ref[0], w_ref[A_WIDTH + B_WIDTH:, :], preferred_element_type=F32)
    o_ref[0] = x_ref[0] + g_ref[0] * y


def _out_projection(x, ya, yb, yc, w, gate, tm):
    bsz, t, d = x.shape
    blk = lambda n: pl.BlockSpec((1, tm, n), lambda b, i: (b, i, 0))
    return pl.pallas_call(
        _outproj_kernel,
        grid=(bsz, t // tm),
        in_specs=[blk(d), blk(A_WIDTH), blk(B_WIDTH), blk(C_WIDTH),
                  pl.BlockSpec((d, d), lambda b, i: (0, 0)),
                  pl.BlockSpec((1, 1, d), lambda b, i: (b, 0, 0))],
        out_specs=blk(d),
        out_shape=jax.ShapeDtypeStruct((bsz, t, d), F32),
        compiler_params=_params("arbitrary", "arbitrary"),
        name="out_projection",
    )(x, ya, yb, yc, w, gate)


def _mlp_kernel(x_ref, g_ref, sc_ref, sh_ref, gate_ref, w1_ref, w2_ref, o_ref, h_ref, acc_ref):
    f = pl.program_id(2)

    @pl.when(f == 0)
    def _():
        h_ref[...] = _norm_mod(x_ref[0], g_ref[...], sc_ref[0], sh_ref[0]).astype(BF16)
        acc_ref[...] = jnp.zeros_like(acc_ref)

    a = jnp.dot(h_ref[...], w1_ref[...], preferred_element_type=F32)
    a = jnp.maximum(a, 0.0)
    acc_ref[...] += jnp.dot((a * a).astype(BF16), w2_ref[...], preferred_element_type=F32)

    @pl.when(f == pl.num_programs(2) - 1)
    def _():
        o_ref[0] = x_ref[0] + gate_ref[0] * acc_ref[...]


def _mlp(x, g, sc, sh, gate, w1, w2, tm, tf):
    bsz, t, d = x.shape
    dff = w1.shape[1]
    row = pl.BlockSpec((1, 1, d), lambda b, i, f: (b, 0, 0))
    return pl.pallas_call(
        _mlp_kernel,
        grid=(bsz, t // tm, dff // tf),
        in_specs=[
            pl.BlockSpec((1, tm, d), lambda b, i, f: (b, i, 0)),
            pl.BlockSpec((1, d), lambda b, i, f: (0, 0)),
            row, row, row,
            pl.BlockSpec((d, tf), lambda b, i, f: (0, f)),
            pl.BlockSpec((tf, d), lambda b, i, f: (f, 0)),
        ],
        out_specs=pl.BlockSpec((1, tm, d), lambda b, i, f: (b, i, 0)),
        out_shape=jax.ShapeDtypeStruct((bsz, t, d), F32),
        scratch_shapes=[pltpu.VMEM((tm, d), BF16), pltpu.VMEM((tm, d), F32)],
        compiler_params=_params("arbitrary", "arbitrary", "arbitrary"),
        name="relu2_mlp",
    )(x, g, sc, sh, gate, w1, w2)


def _alibi_slopes():
    s = jnp.asarray([2.0 ** (-8.0 * (h + 1) / B_HEADS) for h in range(B_HEADS)], F32)
    return jnp.broadcast_to(s[:, None, None], (B_HEADS, 1, V7X_LANES))


def _pad_rows(w, before, after):
    return jnp.pad(w, ((before, after), (0, 0)))


def _layer(x, mod, p, lambda_init, tiles):
    bsz = x.shape[0]
    sh1, sc1, g1, sh2, sc2, g2 = [m.reshape(bsz, 1, D_MODEL) for m in jnp.split(mod, 6, axis=-1)]
    row = lambda a: a.reshape(1, -1)

    proj = _in_projection(x, row(p["norm1_g"]), sc1, sh1, p["w_in"].astype(BF16), tiles["proj"])
    y_a = _rwkv_mix(
        proj, row(p["tshift_mu"]), row(p["decay_w0"]),
        _pad_rows(p["decay_w2"], 0, ICLR_LORA).astype(BF16), row(p["iclr_a0"]),
        _pad_rows(p["iclr_a2"], DECAY_LORA, 0).astype(BF16), p["gate_g2"].astype(BF16),
        row(p["k_k"]), row(p["k_a"]), row(p["r_k"]), row(p["lnx_w"]), row(p["lnx_b"]),
        tiles["rwkv"])
    two = lambda a: row(jnp.concatenate([a, a]))
    y_b = _diff_attention(
        proj, two(p["q_norm_g"]), two(p["k_norm_g"]), row(p["subln_g"]),
        row(p["lambda_q1"]), row(p["lambda_k1"]), row(p["lambda_q2"]), row(p["lambda_k2"]),
        _alibi_slopes(), lambda_init)
    y_c = _conformer_conv(proj, p["conv_w"], row(p["conv_b"]), row(p["conv_ln_w"]),
                          row(p["conv_ln_b"]), tiles["conv"])
    x = _out_projection(x, y_a, y_b, y_c, p["w_out"].astype(BF16), g1, tiles["proj"])
    return _mlp(x, row(p["norm2_g"]), sc2, sh2, g2, p["mlp_w1"].astype(BF16),
                p["mlp_w2"].astype(BF16), tiles["mlp_tm"], tiles["mlp_tf"])


_LAYER_PARAMS = ("norm1_g", "w_in", "tshift_mu", "decay_w0", "decay_w2", "iclr_a0", "iclr_a2",
                 "gate_g2", "k_k", "k_a", "r_k", "lnx_w", "lnx_b", "q_norm_g", "k_norm_g",
                 "lambda_q1", "lambda_k1", "lambda_q2", "lambda_k2", "subln_g", "conv_w",
                 "conv_b", "conv_ln_w", "conv_ln_b", "w_out", "norm2_g", "mlp_w1", "mlp_w2")


def _block(x, c, ada_w, ada_b, params, tiles):
    mod = _ada_modulation(c, ada_w, ada_b)
    for i in range(ada_w.shape[0]):
        lambda_init = 0.8 - 0.6 * math.exp(-0.3 * i)
        x = _layer(x, mod[i], {k: v[i] for k, v in params.items()}, lambda_init, tiles)
    return x


def kernel(x, c, ada_w, ada_b, norm1_g, w_in, tshift_mu, decay_w0, decay_w2, iclr_a0, iclr_a2,
           gate_g2, k_k, k_a, r_k, lnx_w, lnx_b, q_norm_g, k_norm_g, lambda_q1, lambda_k1,
           lambda_q2, lambda_k2, subln_g, conv_w, conv_b, conv_ln_w, conv_ln_b, w_out, norm2_g,
           mlp_w1, mlp_w2):
    values = (norm1_g, w_in, tshift_mu, decay_w0, decay_w2, iclr_a0, iclr_a2, gate_g2, k_k, k_a,
              r_k.reshape(r_k.shape[0], -1), lnx_w, lnx_b, q_norm_g, k_norm_g, lambda_q1,
              lambda_k1, lambda_q2, lambda_k2, subln_g, conv_w, conv_b, conv_ln_w, conv_ln_b,
              w_out, norm2_g, mlp_w1, mlp_w2)
    t = x.shape[1]
    tiles = dict(proj=min(PROJ_TM, t), rwkv=min(RWKV_TILE, t), conv=min(CONV_TILE, t),
                 mlp_tm=min(MLP_TM, t), mlp_tf=MLP_TF)
    return _block(x, c, ada_w, ada_b, dict(zip(_LAYER_PARAMS, values)), tiles)
```

```python
import functools
import math

import jax
import jax.numpy as jnp
from jax import lax
from jax.experimental import pallas as pl
from jax.experimental.pallas import tpu as pltpu

F32 = jnp.float32
BF16 = jnp.bfloat16

D_MODEL = 1024
DEPTH = 2
CHUNK = 64
A_HEADS = 4
A_HEAD_DIM = 64
A_WIDTH = A_HEADS * A_HEAD_DIM
DECAY_LORA = 64
ICLR_LORA = 64
GATE_LORA = 128
A_COLS = 3 * A_WIDTH + DECAY_LORA + ICLR_LORA + GATE_LORA
B_HEADS = 4
B_QK_DIM = 64
B_V_DIM = 2 * B_QK_DIM
B_WIDTH = B_HEADS * B_V_DIM
B_QK_COLS = B_HEADS * 2 * B_QK_DIM
B_COLS = 2 * B_QK_COLS + B_WIDTH
C_WIDTH = D_MODEL - A_WIDTH - B_WIDTH
CONV_WIDTH = 31
C_COLS = 2 * C_WIDTH
N_IN = A_COLS + B_COLS + C_COLS
D_FF = 4 * D_MODEL
RMS_EPS = 1e-6
LN_EPS = 1e-5
LNX_EPS = 64e-5
NEG_INF = -1e30
EXP_NEG_HALF = math.exp(-0.5)

V7X_LANES = 128
V7X_SUBLANES = 8
V7X_VMEM_LIMIT_BYTES = 56 * 1024 * 1024

RWKV_CHUNK = 64
RWKV_TILE = 256
RWKV_ROWS = 2
ATTN_TQ = 512
CONV_TILE = 256
CONV_HALO = 32
PROJ_TM = 512
MLP_TM = 1024
MLP_TF = 1024


def _params(*sem):
    return pltpu.CompilerParams(dimension_semantics=sem,
                                vmem_limit_bytes=V7X_VMEM_LIMIT_BYTES)


def _dot(a, b):
    return jnp.dot(a.astype(BF16), b.astype(BF16), preferred_element_type=F32)


def _dot_nt(a, b):
    return lax.dot_general(a.astype(BF16), b.astype(BF16), (((1,), (1,)), ((), ())),
                           preferred_element_type=F32)


def _dot_tn(a, b):
    return lax.dot_general(a.astype(BF16), b.astype(BF16), (((0,), (0,)), ((), ())),
                           preferred_element_type=F32)


def _split_dot(a, b_exact, parts):
    out = None
    rem = a
    for _ in range(parts):
        piece = rem.astype(BF16)
        term = jnp.dot(piece, b_exact, preferred_element_type=F32)
        out = term if out is None else out + term
        rem = rem - piece.astype(F32)
    return out


def _sigmoid(x):
    return 1.0 / (1.0 + jnp.exp(-x))


def _ada_kernel(c_ref, w_ref, b_ref, o_ref):
    c = c_ref[...]
    cond = c * _sigmoid(c)
    o_ref[0] = _dot(cond, w_ref[0]) + b_ref[0]


def _ada_modulation(c, ada_w, ada_b):
    depth, d, n = ada_w.shape
    bsz = c.shape[0]
    tn = 1536
    return pl.pallas_call(
        _ada_kernel,
        grid=(depth, n // tn),
        in_specs=[
            pl.BlockSpec((bsz, d), lambda l, j: (0, 0)),
            pl.BlockSpec((1, d, tn), lambda l, j: (l, 0, j)),
            pl.BlockSpec((1, 1, tn), lambda l, j: (l, 0, j)),
        ],
        out_specs=pl.BlockSpec((1, bsz, tn), lambda l, j: (l, 0, j)),
        out_shape=jax.ShapeDtypeStruct((depth, bsz, n), F32),
        compiler_params=_params("arbitrary", "arbitrary"),
        name="ada_modulation",
    )(c, ada_w, ada_b.reshape(depth, 1, n))


def _norm_mod(x, g, sc, sh):
    ms = jnp.mean(x * x, axis=-1, keepdims=True)
    y = x * lax.rsqrt(ms + RMS_EPS) * g
    return y * (1.0 + sc) + sh


def _inproj_kernel(x_ref, g_ref, sc_ref, sh_ref, w_ref, o_ref, *, tn):
    h = _norm_mod(x_ref[0], g_ref[...], sc_ref[0], sh_ref[0]).astype(BF16)
    n = w_ref.shape[1]
    for j in range(n // tn):
        o_ref[0, :, j * tn:(j + 1) * tn] = jnp.dot(
            h, w_ref[:, j * tn:(j + 1) * tn], preferred_element_type=F32)


def _in_projection(x, g, sc, sh, w, tm):
    bsz, t, d = x.shape
    n = w.shape[1]
    return pl.pallas_call(
        functools.partial(_inproj_kernel, tn=1024),
        grid=(bsz, t // tm),
        in_specs=[
            pl.BlockSpec((1, tm, d), lambda b, i: (b, i, 0)),
            pl.BlockSpec((1, d), lambda b, i: (0, 0)),
            pl.BlockSpec((1, 1, d), lambda b, i: (b, 0, 0)),
            pl.BlockSpec((1, 1, d), lambda b, i: (b, 0, 0)),
            pl.BlockSpec((d, n), lambda b, i: (0, 0)),
        ],
        out_specs=pl.BlockSpec((1, tm, n), lambda b, i: (b, i, 0)),
        out_shape=jax.ShapeDtypeStruct((bsz, t, n), F32),
        compiler_params=_params("arbitrary", "arbitrary"),
        name="in_projection",
    )(x, g, sc, sh, w)


def _rwkv_kernel(pa_ref, mu_ref, w0_ref, w2_ref, a0_ref, a2_ref, g2_ref, kk_ref, ka_ref,
                 rk_ref, lw_ref, lb_ref, o_ref, carry_ref, s_ref, *, rows, tile, chunk):
    aw = A_WIDTH
    hd = A_HEAD_DIM
    n_chunks = tile // chunk

    @pl.when(pl.program_id(1) == 0)
    def _():
        carry_ref[...] = jnp.zeros_like(carry_ref)
        s_ref[...] = jnp.zeros_like(s_ref)

    ri = lax.broadcasted_iota(jnp.int32, (aw, aw), 0)
    ci = lax.broadcasted_iota(jnp.int32, (aw, aw), 1)
    same_head = (ri // hd) == (ci // hd)
    head_ones = same_head.astype(F32).astype(BF16)
    eye_full = ri == ci
    lane_head = lax.broadcasted_iota(jnp.int32, (hd, aw), 1) // hd
    t_c = lax.broadcasted_iota(jnp.int32, (chunk, aw), 0)
    s_c = lax.broadcasted_iota(jnp.int32, (chunk, aw), 1) % chunk
    strict = s_c < t_c
    incl = s_c <= t_c
    eye_c = (s_c == t_c).astype(F32)
    ti = lax.broadcasted_iota(jnp.int32, (tile, tile), 0)
    si = lax.broadcasted_iota(jnp.int32, (tile, tile), 1)
    tri = ((si <= ti) & ((si // chunk) == (ti // chunk))).astype(F32).astype(BF16)
    row = lax.broadcasted_iota(jnp.int32, (tile, 1), 0)

    def head_sum(x):
        return _split_dot(x, head_ones, 2)

    def blockdiag(x):
        return jnp.where(same_head, jnp.concatenate([x] * A_HEADS, axis=0), 0.0)

    seq = []
    for bi in range(rows):
        pa = pa_ref[bi]
        crow = bi * V7X_SUBLANES
        prev = jnp.where(row == 0, carry_ref[crow:crow + 1, :], pltpu.roll(pa, 1, 0))
        carry_ref[crow:crow + 1, :] = pa[tile - 1:tile, :]
        xs = pa + mu_ref[...] * (prev - pa)
        r = xs[:, 0:aw]
        k = xs[:, aw:2 * aw]
        v = xs[:, 2 * aw:3 * aw]
        lora = xs[:, 3 * aw:3 * aw + DECAY_LORA + ICLR_LORA]
        dg = xs[:, 3 * aw + DECAY_LORA + ICLR_LORA:]
        u = w0_ref[...] + _dot(jnp.tanh(lora), w2_ref[...])
        logdecay = -EXP_NEG_HALF * _sigmoid(u)
        iclr = _sigmoid(a0_ref[...] + _dot(lora, a2_ref[...]))
        gate = _dot(_sigmoid(dg), g2_ref[...])
        kk = k * kk_ref[...]
        kk = kk / jnp.maximum(jnp.sqrt(head_sum(kk * kk)), 1e-12)
        k = k * (1.0 + (iclr - 1.0) * ka_ref[...])
        b = kk * iclr
        cum = None
        rem = logdecay
        for _ in range(3):
            piece = rem.astype(BF16)
            term = jnp.dot(tri, piece, preferred_element_type=F32)
            cum = term if cum is None else cum + term
            rem = rem - piece.astype(F32)
        seq.append(dict(r=r, k=k, v=v, kk=kk, b=b, ld=logdecay, cum=cum, gate=gate))

    chains = [(bi, c) for bi in range(rows) for c in range(n_chunks)]
    ch = []
    for bi, c in chains:
        q = seq[bi]
        sl = slice(c * chunk, (c + 1) * chunk)
        cum_c = q["cum"][sl]
        total = cum_c[chunk - 1:chunk, :]
        w_inv = jnp.exp(-cum_c)
        w_end = jnp.exp(total - cum_c)
        ch.append(dict(
            rt=q["r"][sl] * jnp.exp(cum_c), at=-q["kk"][sl] * jnp.exp(cum_c - q["ld"][sl]),
            bt=q["b"][sl] * w_inv, kt=q["k"][sl] * w_inv, be=q["b"][sl] * w_end,
            ke=q["k"][sl] * w_end, vc=q["v"][sl], total=total))
    for d in ch:
        lhs = jnp.concatenate([d["at"], d["rt"]], axis=0)
        rhs = jnp.concatenate([blockdiag(d["bt"]), blockdiag(d["kt"])], axis=0)
        prod = _dot_nt(lhs, rhs)
        d["low"] = jnp.where(strict, prod[:chunk, :aw], 0.0)
        d["a_ak"] = jnp.where(strict, prod[:chunk, aw:], 0.0)
        d["a_rb"] = jnp.where(incl, prod[chunk:, :aw], 0.0)
        d["a_rk"] = jnp.where(incl, prod[chunk:, aw:], 0.0)
    for d in ch:
        d["pw"] = _dot(d["low"], blockdiag(d["low"]))
        d["ti"] = eye_c + d["low"]
        d["akv"] = _dot(d["a_ak"], blockdiag(d["vc"]))
        d["y_in"] = _dot(d["a_rk"], blockdiag(d["vc"]))
    for _ in range(int(math.log2(chunk)) - 2):
        for d in ch:
            both = _dot(jnp.concatenate([d["pw"], d["ti"]], axis=0), blockdiag(d["pw"]))
            d["pw"] = both[:chunk]
            d["ti"] = d["ti"] + both[chunk:]
    for d in ch:
        d["ti"] = d["ti"] + _dot(d["ti"], blockdiag(d["pw"]))
    for d in ch:
        both = _dot(d["ti"], jnp.concatenate([blockdiag(d["at"]), blockdiag(d["akv"])], axis=1))
        d["ah"] = both[:, :aw]
        d["vh"] = both[:, aw:]
    for d in ch:
        both = _dot(d["a_rb"], jnp.concatenate([blockdiag(d["ah"]), blockdiag(d["vh"])], axis=1))
        d["rh"] = d["rt"] + both[:, :aw]
        d["y_in"] = d["y_in"] + both[:, aw:]
        trans = jnp.where(same_head, _dot_tn(d["ah"], d["be"]), 0.0)
        d["trans"] = trans + jnp.where(eye_full, jnp.exp(d["total"]), 0.0)
        add_full = _dot_tn(jnp.concatenate([d["vh"], d["vc"]], axis=0),
                           jnp.concatenate([d["be"], d["ke"]], axis=0))
        add = None
        for h in range(A_HEADS):
            blk = jnp.where(lane_head == h, add_full[h * hd:(h + 1) * hd], 0.0)
            add = blk if add is None else add + blk
        d["add"] = add

    states = [s_ref[bi * hd:(bi + 1) * hd, :] for bi in range(rows)]
    ys = [[] for _ in range(rows)]
    for c in range(n_chunks):
        for bi in range(rows):
            d = ch[bi * n_chunks + c]
            ys[bi].append(_dot_nt(d["rh"], blockdiag(states[bi])) + d["y_in"])
            states[bi] = _dot(states[bi], d["trans"]) + d["add"]

    inv_n = 1.0 / hd
    for bi in range(rows):
        q = seq[bi]
        s_ref[bi * hd:(bi + 1) * hd, :] = states[bi]
        y = jnp.concatenate(ys[bi], axis=0)
        mu_h = head_sum(y) * inv_n
        yc = y - mu_h
        var_h = head_sum(yc * yc) * inv_n
        yn = yc * lax.rsqrt(var_h + LNX_EPS) * lw_ref[...] + lb_ref[...]
        bonus = head_sum(q["r"] * q["k"] * rk_ref[...]) * q["v"]
        o_ref[bi] = ((yn + bonus) * q["gate"]).astype(o_ref.dtype)


def _rwkv_mix(proj, mu, w0, w2p, a0, a2p, g2, k_k, k_a, r_k, lnx_w, lnx_b, tile, rows):
    bsz, t, _ = proj.shape
    vec = lambda n: pl.BlockSpec((1, n), lambda b, i: (0, 0))
    mat = lambda m, n: pl.BlockSpec((m, n), lambda b, i: (0, 0))
    return pl.pallas_call(
        functools.partial(_rwkv_kernel, rows=rows, tile=tile, chunk=RWKV_CHUNK),
        grid=(bsz // rows, t // tile),
        in_specs=[
            pl.BlockSpec((rows, tile, A_COLS), lambda b, i: (b, i, 0)),
            vec(A_COLS), vec(A_WIDTH), mat(DECAY_LORA + ICLR_LORA, A_WIDTH),
            vec(A_WIDTH), mat(DECAY_LORA + ICLR_LORA, A_WIDTH), mat(GATE_LORA, A_WIDTH),
            vec(A_WIDTH), vec(A_WIDTH), vec(A_WIDTH), vec(A_WIDTH), vec(A_WIDTH),
        ],
        out_specs=pl.BlockSpec((rows, tile, A_WIDTH), lambda b, i: (b, i, 0)),
        out_shape=jax.ShapeDtypeStruct((bsz, t, A_WIDTH), BF16),
        scratch_shapes=[pltpu.VMEM((rows * V7X_SUBLANES, A_COLS), F32),
                        pltpu.VMEM((rows * A_HEAD_DIM, A_WIDTH), F32)],
        compiler_params=_params("arbitrary", "arbitrary"),
        name="rwkv7_mix",
    )(proj, mu, w0, w2p, a0, a2p, g2, k_k, k_a, r_k, lnx_w, lnx_b)


LOG2E = math.log2(math.e)
ALIBI_PIECES = 3


def _half_rms(x, gain, scale):
    lane = lax.broadcasted_iota(jnp.int32, (1, x.shape[1]), 1)
    lo = lane < B_QK_DIM
    x2 = x * x
    s_lo = jnp.sum(jnp.where(lo, x2, 0.0), axis=-1, keepdims=True)
    s_hi = jnp.sum(jnp.where(lo, 0.0, x2), axis=-1, keepdims=True)
    ms = jnp.where(lo, s_lo, s_hi) * (1.0 / B_QK_DIM)
    return x * lax.rsqrt(ms + RMS_EPS) * (gain * scale)


def _attn_kernel(q_ref, k_ref, v_ref, qg_ref, kg_ref, sg_ref, lq1_ref, lk1_ref, lq2_ref,
                 lk2_ref, slope_ref, o_ref, kx_ref, vb_ref, corr_ref, *, tq, lambda_init):
    i = pl.program_id(2)
    t_all = k_ref.shape[1]
    slope2 = slope_ref[0][:, 0:1] * LOG2E

    @pl.when(i == 0)
    def _():
        kx_ref[:, 0:B_V_DIM] = _half_rms(k_ref[0], kg_ref[...], 1.0).astype(BF16)
        pos = lax.broadcasted_iota(jnp.int32, (t_all, V7X_LANES), 0).astype(F32)
        lane = lax.broadcasted_iota(jnp.int32, (t_all, V7X_LANES), 1)
        rem = pos * slope2
        ext = jnp.zeros((t_all, V7X_LANES), F32)
        for piece in range(ALIBI_PIECES):
            part = rem.astype(BF16).astype(F32)
            ext = jnp.where(lane == piece, part, ext)
            rem = rem - part
        kx_ref[:, B_V_DIM:] = ext.astype(BF16)
        vb_ref[...] = v_ref[0].astype(BF16)
        r = lax.broadcasted_iota(jnp.int32, (tq, tq), 0)
        c = lax.broadcasted_iota(jnp.int32, (tq, tq), 1)
        after = jnp.where(c > r, (c - r).astype(F32) * (-2.0 * slope2), 0.0)
        corr_ref[...] = jnp.where((c // CHUNK) <= (r // CHUNK), after, NEG_INF)

    qn = _half_rms(q_ref[0], qg_ref[...], LOG2E * B_QK_DIM ** -0.5)
    lane = lax.broadcasted_iota(jnp.int32, (tq, V7X_LANES), 1)
    lo = lane < B_QK_DIM
    ones = (lane < ALIBI_PIECES).astype(F32)
    qx = jnp.concatenate(
        [jnp.concatenate([jnp.where(lo, qn, 0.0), ones], axis=1),
         jnp.concatenate([jnp.where(lo, 0.0, qn), ones], axis=1)], axis=0).astype(BF16)

    def scores(start):
        return lax.dot_general(qx, kx_ref[pl.ds(start, tq), :], (((1,), (1,)), ((), ())),
                               preferred_element_type=F32)

    def update(s, carry, start):
        m, l, acc = carry
        m_new = jnp.maximum(m, jnp.max(s, axis=-1, keepdims=True))
        alpha = jnp.exp2(m - m_new)
        p = jnp.exp2(s - m_new)
        l = alpha * l + jnp.sum(p, axis=-1, keepdims=True)
        acc = alpha * acc + jnp.dot(p.astype(BF16), vb_ref[pl.ds(start, tq), :],
                                    preferred_element_type=F32)
        return m_new, l, acc

    def body(j, carry):
        start = pl.multiple_of(j * tq, tq)
        return update(scores(start), carry, start)

    init = (jnp.full((2 * tq, 1), NEG_INF, F32), jnp.zeros((2 * tq, 1), F32),
            jnp.zeros((2 * tq, B_V_DIM), F32))
    carry = lax.fori_loop(0, i, body, init)
    start = pl.multiple_of(i * tq, tq)
    corr = corr_ref[...]
    _, l, acc = update(scores(start) + jnp.concatenate([corr, corr], axis=0), carry, start)

    o = acc / l
    lam = (jnp.exp(jnp.sum(lq1_ref[...] * lk1_ref[...], axis=-1, keepdims=True))
           - jnp.exp(jnp.sum(lq2_ref[...] * lk2_ref[...], axis=-1, keepdims=True))
           + lambda_init)
    d = o[:tq] - lam * o[tq:]
    ms = jnp.mean(d * d, axis=-1, keepdims=True)
    out = d * lax.rsqrt(ms + RMS_EPS) * sg_ref[...] * (1.0 - lambda_init)
    o_ref[0] = out.astype(o_ref.dtype)


def _diff_attention(proj, qg2, kg2, sg, lq1, lk1, lq2, lk2, slopes, lambda_init, tq):
    bsz, t, _ = proj.shape
    q_blk = A_COLS // B_V_DIM
    k_blk = (A_COLS + B_QK_COLS) // B_V_DIM
    v_blk = (A_COLS + 2 * B_QK_COLS) // B_V_DIM
    vec = lambda n: pl.BlockSpec((1, n), lambda b, h, i: (0, 0))
    return pl.pallas_call(
        functools.partial(_attn_kernel, tq=tq, lambda_init=lambda_init),
        grid=(bsz, B_HEADS, t // tq),
        in_specs=[
            pl.BlockSpec((1, tq, B_V_DIM), lambda b, h, i: (b, i, q_blk + h)),
            pl.BlockSpec((1, t, B_V_DIM), lambda b, h, i: (b, 0, k_blk + h)),
            pl.BlockSpec((1, t, B_V_DIM), lambda b, h, i: (b, 0, v_blk + h)),
            vec(B_V_DIM), vec(B_V_DIM), vec(B_V_DIM),
            vec(B_QK_DIM), vec(B_QK_DIM), vec(B_QK_DIM), vec(B_QK_DIM),
            pl.BlockSpec((1, 1, V7X_LANES), lambda b, h, i: (h, 0, 0)),
        ],
        out_specs=pl.BlockSpec((1, tq, B_V_DIM), lambda b, h, i: (b, i, h)),
        out_shape=jax.ShapeDtypeStruct((bsz, t, B_WIDTH), BF16),
        scratch_shapes=[pltpu.VMEM((t, B_V_DIM + V7X_LANES), BF16),
                        pltpu.VMEM((t, B_V_DIM), BF16),
                        pltpu.VMEM((tq, tq), F32)],
        compiler_params=_params("arbitrary", "arbitrary", "arbitrary"),
        name="diff_attention",
    )(proj, proj, proj, qg2, kg2, sg, lq1, lk1, lq2, lk2, slopes)


def _conv_kernel(pc_ref, w_ref, b_ref, lw_ref, lb_ref, o_ref, hbuf_ref, *, tile, rows):
    @pl.when(pl.program_id(1) == 0)
    def _():
        hbuf_ref[0, 0:CONV_HALO, :] = jnp.zeros((CONV_HALO, C_WIDTH), F32)

    pc = pc_ref[0]
    n = CONV_HALO + tile
    hbuf_ref[0, CONV_HALO:n, :] = pc[:, :C_WIDTH] * _sigmoid(pc[:, C_WIDTH:])
    full = hbuf_ref[0]
    for s in range(1, V7X_SUBLANES):
        hbuf_ref[s] = pltpu.roll(full, n - s, 0)
    lead = CONV_HALO - (CONV_WIDTH - 1)
    for c in range(tile // rows):
        acc = jnp.zeros((rows, C_WIDTH), F32)
        for j in range(CONV_WIDTH):
            start = c * rows + lead + j
            s = start % V7X_SUBLANES
            acc = acc + w_ref[j:j + 1, :] * hbuf_ref[s, start - s:start - s + rows, :]
        acc = acc + b_ref[...]
        mu = jnp.mean(acc, axis=-1, keepdims=True)
        d = acc - mu
        var = jnp.mean(d * d, axis=-1, keepdims=True)
        y = d * lax.rsqrt(var + LN_EPS) * lw_ref[...] + lb_ref[...]
        o_ref[0, c * rows:(c + 1) * rows, :] = (y * _sigmoid(y)).astype(o_ref.dtype)
    hbuf_ref[0, 0:CONV_HALO, :] = hbuf_ref[0, tile:n, :]


def _conformer_conv(proj, conv_w, conv_b, ln_w, ln_b, tile):
    bsz, t, _ = proj.shape
    c_blk = (A_COLS + B_COLS) // C_COLS
    vec = lambda n: pl.BlockSpec((1, n), lambda b, i: (0, 0))
    return pl.pallas_call(
        functools.partial(_conv_kernel, tile=tile, rows=64),
        grid=(bsz, t // tile),
        in_specs=[
            pl.BlockSpec((1, tile, C_COLS), lambda b, i: (b, i, c_blk)),
            pl.BlockSpec((CONV_WIDTH, C_WIDTH), lambda b, i: (0, 0)),
            vec(C_WIDTH), vec(C_WIDTH), vec(C_WIDTH),
        ],
        out_specs=pl.BlockSpec((1, tile, C_WIDTH), lambda b, i: (b, i, 0)),
        out_shape=jax.ShapeDtypeStruct((bsz, t, C_WIDTH), BF16),
        scratch_shapes=[pltpu.VMEM((V7X_SUBLANES, CONV_HALO + tile, C_WIDTH), F32)],
        compiler_params=_params("arbitrary", "arbitrary"),
        name="conformer_conv",
    )(proj, conv_w, conv_b, ln_w, ln_b)


def _outproj_kernel(x_ref, ya_ref, yb_ref, yc_ref, w_ref, g_ref, o_ref):
    y = jnp.dot(ya_ref[0], w_ref[0:A_WIDTH, :], preferred_element_type=F32)
    y = y + jnp.dot(yb_ref[0], w_ref[A_WIDTH:A_WIDTH + B_WIDTH, :], preferred_element_type=F32)
    y = y + jnp.dot(yc_ref[0], w_ref[A_WIDTH + B_WIDTH:, :], preferred_element_type=F32)
    o_ref[0] = x_ref[0] + g_ref[0] * y


def _out_projection(x, ya, yb, yc, w, gate, tm):
    bsz, t, d = x.shape
    blk = lambda n: pl.BlockSpec((1, tm, n), lambda b, i: (b, i, 0))
    return pl.pallas_call(
        _outproj_kernel,
        grid=(bsz, t // tm),
        in_specs=[blk(d), blk(A_WIDTH), blk(B_WIDTH), blk(C_WIDTH),
                  pl.BlockSpec((d, d), lambda b, i: (0, 0)),
                  pl.BlockSpec((1, 1, d), lambda b, i: (b, 0, 0))],
        out_specs=blk(d),
        out_shape=jax.ShapeDtypeStruct((bsz, t, d), F32),
        compiler_params=_params("arbitrary", "arbitrary"),
        name="out_projection",
    )(x, ya, yb, yc, w, gate)


def _mlp_kernel(x_ref, g_ref, sc_ref, sh_ref, gate_ref, w1_ref, w2_ref, o_ref, h_ref, acc_ref):
    f = pl.program_id(2)

    @pl.when(f == 0)
    def _():
        h_ref[...] = _norm_mod(x_ref[0], g_ref[...], sc_ref[0], sh_ref[0]).astype(BF16)
        acc_ref[...] = jnp.zeros_like(acc_ref)

    a = jnp.dot(h_ref[...], w1_ref[...], preferred_element_type=F32)
    a = jnp.maximum(a, 0.0)
    acc_ref[...] += jnp.dot((a * a).astype(BF16), w2_ref[...], preferred_element_type=F32)

    @pl.when(f == pl.num_programs(2) - 1)
    def _():
        o_ref[0] = x_ref[0] + gate_ref[0] * acc_ref[...]


def _mlp(x, g, sc, sh, gate, w1, w2, tm, tf):
    bsz, t, d = x.shape
    dff = w1.shape[1]
    row = pl.BlockSpec((1, 1, d), lambda b, i, f: (b, 0, 0))
    return pl.pallas_call(
        _mlp_kernel,
        grid=(bsz, t // tm, dff // tf),
        in_specs=[
            pl.BlockSpec((1, tm, d), lambda b, i, f: (b, i, 0)),
            pl.BlockSpec((1, d), lambda b, i, f: (0, 0)),
            row, row, row,
            pl.BlockSpec((d, tf), lambda b, i, f: (0, f)),
            pl.BlockSpec((tf, d), lambda b, i, f: (f, 0)),
        ],
        out_specs=pl.BlockSpec((1, tm, d), lambda b, i, f: (b, i, 0)),
        out_shape=jax.ShapeDtypeStruct((bsz, t, d), F32),
        scratch_shapes=[pltpu.VMEM((tm, d), BF16), pltpu.VMEM((tm, d), F32)],
        compiler_params=_params("arbitrary", "arbitrary", "arbitrary"),
        name="relu2_mlp",
    )(x, g, sc, sh, gate, w1, w2)


def _alibi_slopes():
    s = jnp.asarray([2.0 ** (-8.0 * (h + 1) / B_HEADS) for h in range(B_HEADS)], F32)
    return jnp.broadcast_to(s[:, None, None], (B_HEADS, 1, V7X_LANES))


def _pad_rows(w, before, after):
    return jnp.pad(w, ((before, after), (0, 0)))


def _layer(x, mod, p, lambda_init, tiles):
    bsz = x.shape[0]
    sh1, sc1, g1, sh2, sc2, g2 = [m.reshape(bsz, 1, D_MODEL) for m in jnp.split(mod, 6, axis=-1)]
    row = lambda a: a.reshape(1, -1)

    proj = _in_projection(x, row(p["norm1_g"]), sc1, sh1, p["w_in"].astype(BF16), tiles["proj"])
    y_a = _rwkv_mix(
        proj, row(p["tshift_mu"]), row(p["decay_w0"]),
        _pad_rows(p["decay_w2"], 0, ICLR_LORA).astype(BF16), row(p["iclr_a0"]),
        _pad_rows(p["iclr_a2"], DECAY_LORA, 0).astype(BF16), p["gate_g2"].astype(BF16),
        row(p["k_k"]), row(p["k_a"]), row(p["r_k"]), row(p["lnx_w"]), row(p["lnx_b"]),
        tiles["rwkv"], RWKV_ROWS)
    two = lambda a: row(jnp.concatenate([a, a]))
    y_b = _diff_attention(
        proj, two(p["q_norm_g"]), two(p["k_norm_g"]), row(p["subln_g"]),
        row(p["lambda_q1"]), row(p["lambda_k1"]), row(p["lambda_q2"]), row(p["lambda_k2"]),
        _alibi_slopes(), lambda_init, tiles["attn"])
    y_c = _conformer_conv(proj, p["conv_w"], row(p["conv_b"]), row(p["conv_ln_w"]),
                          row(p["conv_ln_b"]), tiles["conv"])
    x = _out_projection(x, y_a, y_b, y_c, p["w_out"].astype(BF16), g1, tiles["proj"])
    return _mlp(x, row(p["norm2_g"]), sc2, sh2, g2, p["mlp_w1"].astype(BF16),
                p["mlp_w2"].astype(BF16), tiles["mlp_tm"], tiles["mlp_tf"])


_LAYER_PARAMS = ("norm1_g", "w_in", "tshift_mu", "decay_w0", "decay_w2", "iclr_a0", "iclr_a2",
                 "gate_g2", "k_k", "k_a", "r_k", "lnx_w", "lnx_b", "q_norm_g", "k_norm_g",
                 "lambda_q1", "lambda_k1", "lambda_q2", "lambda_k2", "subln_g", "conv_w",
                 "conv_b", "conv_ln_w", "conv_ln_b", "w_out", "norm2_g", "mlp_w1", "mlp_w2")


def _block(x, c, ada_w, ada_b, params, tiles):
    mod = _ada_modulation(c, ada_w, ada_b)
    for i in range(ada_w.shape[0]):
        lambda_init = 0.8 - 0.6 * math.exp(-0.3 * i)
        x = _layer(x, mod[i], {k: v[i] for k, v in params.items()}, lambda_init, tiles)
    return x


def kernel(x, c, ada_w, ada_b, norm1_g, w_in, tshift_mu, decay_w0, decay_w2, iclr_a0, iclr_a2,
           gate_g2, k_k, k_a, r_k, lnx_w, lnx_b, q_norm_g, k_norm_g, lambda_q1, lambda_k1,
           lambda_q2, lambda_k2, subln_g, conv_w, conv_b, conv_ln_w, conv_ln_b, w_out, norm2_g,
           mlp_w1, mlp_w2):
    values = (norm1_g, w_in, tshift_mu, decay_w0, decay_w2, iclr_a0, iclr_a2, gate_g2, k_k, k_a,
              r_k.reshape(r_k.shape[0], -1), lnx_w, lnx_b, q_norm_g, k_norm_g, lambda_q1,
              lambda_k1, lambda_q2, lambda_k2, subln_g, conv_w, conv_b, conv_ln_w, conv_ln_b,
              w_out, norm2_g, mlp_w1, mlp_w2)
    t = x.shape[1]
    tiles = dict(proj=min(PROJ_TM, t), rwkv=min(RWKV_TILE, t), conv=min(CONV_TILE, t),
                 attn=min(ATTN_TQ, t),
                 mlp_tm=min(MLP_TM, t), mlp_tf=MLP_TF)
    return _block(x, c, ada_w, ada_b, dict(zip(_LAYER_PARAMS, values)), tiles)
```

```python
import functools
import math

import jax
import jax.numpy as jnp
from jax import lax
from jax.experimental import pallas as pl
from jax.experimental.pallas import tpu as pltpu

F32 = jnp.float32
BF16 = jnp.bfloat16

D_MODEL = 1024
DEPTH = 2
CHUNK = 64
A_HEADS = 4
A_HEAD_DIM = 64
A_WIDTH = A_HEADS * A_HEAD_DIM
DECAY_LORA = 64
ICLR_LORA = 64
GATE_LORA = 128
A_COLS = 3 * A_WIDTH + DECAY_LORA + ICLR_LORA + GATE_LORA
B_HEADS = 4
B_QK_DIM = 64
B_V_DIM = 2 * B_QK_DIM
B_WIDTH = B_HEADS * B_V_DIM
B_QK_COLS = B_HEADS * 2 * B_QK_DIM
B_COLS = 2 * B_QK_COLS + B_WIDTH
C_WIDTH = D_MODEL - A_WIDTH - B_WIDTH
CONV_WIDTH = 31
C_COLS = 2 * C_WIDTH
N_IN = A_COLS + B_COLS + C_COLS
D_FF = 4 * D_MODEL
RMS_EPS = 1e-6
LN_EPS = 1e-5
LNX_EPS = 64e-5
NEG_INF = -1e30
EXP_NEG_HALF = math.exp(-0.5)

V7X_LANES = 128
V7X_SUBLANES = 8
V7X_VMEM_LIMIT_BYTES = 56 * 1024 * 1024

RWKV_CHUNK = 64
RWKV_TILE = 256
RWKV_ROWS = 2
ATTN_TQ = 512
ATTN_HEADS = 2
CONV_TILE = 256
CONV_HALO = 32
PROJ_TM = 512
MLP_TM = 1024
MLP_TF = 1024


def _params(*sem):
    return pltpu.CompilerParams(dimension_semantics=sem,
                                vmem_limit_bytes=V7X_VMEM_LIMIT_BYTES)


def _dot(a, b):
    return jnp.dot(a.astype(BF16), b.astype(BF16), preferred_element_type=F32)


def _dot_nt(a, b):
    return lax.dot_general(a.astype(BF16), b.astype(BF16), (((1,), (1,)), ((), ())),
                           preferred_element_type=F32)


def _dot_tn(a, b):
    return lax.dot_general(a.astype(BF16), b.astype(BF16), (((0,), (0,)), ((), ())),
                           preferred_element_type=F32)


def _split_dot(a, b_exact, parts):
    out = None
    rem = a
    for _ in range(parts):
        piece = rem.astype(BF16)
        term = jnp.dot(piece, b_exact, preferred_element_type=F32)
        out = term if out is None else out + term
        rem = rem - piece.astype(F32)
    return out


def _sigmoid(x):
    return 1.0 / (1.0 + jnp.exp(-x))


def _ada_kernel(c_ref, w_ref, b_ref, o_ref):
    c = c_ref[...]
    cond = c * _sigmoid(c)
    o_ref[0] = _dot(cond, w_ref[0]) + b_ref[0]


def _ada_modulation(c, ada_w, ada_b):
    depth, d, n = ada_w.shape
    bsz = c.shape[0]
    tn = 1536
    return pl.pallas_call(
        _ada_kernel,
        grid=(depth, n // tn),
        in_specs=[
            pl.BlockSpec((bsz, d), lambda l, j: (0, 0)),
            pl.BlockSpec((1, d, tn), lambda l, j: (l, 0, j)),
            pl.BlockSpec((1, 1, tn), lambda l, j: (l, 0, j)),
        ],
        out_specs=pl.BlockSpec((1, bsz, tn), lambda l, j: (l, 0, j)),
        out_shape=jax.ShapeDtypeStruct((depth, bsz, n), F32),
        compiler_params=_params("arbitrary", "arbitrary"),
        name="ada_modulation",
    )(c, ada_w, ada_b.reshape(depth, 1, n))


def _norm_mod(x, g, sc, sh):
    ms = jnp.mean(x * x, axis=-1, keepdims=True)
    y = x * lax.rsqrt(ms + RMS_EPS) * g
    return y * (1.0 + sc) + sh


def _inproj_kernel(x_ref, g_ref, sc_ref, sh_ref, w_ref, o_ref, *, tn):
    h = _norm_mod(x_ref[0], g_ref[...], sc_ref[0], sh_ref[0]).astype(BF16)
    n = w_ref.shape[1]
    for j in range(n // tn):
        o_ref[0, :, j * tn:(j + 1) * tn] = jnp.dot(
            h, w_ref[:, j * tn:(j + 1) * tn], preferred_element_type=F32)


def _in_projection(x, g, sc, sh, w, tm):
    bsz, t, d = x.shape
    n = w.shape[1]
    return pl.pallas_call(
        functools.partial(_inproj_kernel, tn=1024),
        grid=(bsz, t // tm),
        in_specs=[
            pl.BlockSpec((1, tm, d), lambda b, i: (b, i, 0)),
            pl.BlockSpec((1, d), lambda b, i: (0, 0)),
            pl.BlockSpec((1, 1, d), lambda b, i: (b, 0, 0)),
            pl.BlockSpec((1, 1, d), lambda b, i: (b, 0, 0)),
            pl.BlockSpec((d, n), lambda b, i: (0, 0)),
        ],
        out_specs=pl.BlockSpec((1, tm, n), lambda b, i: (b, i, 0)),
        out_shape=jax.ShapeDtypeStruct((bsz, t, n), F32),
        compiler_params=_params("arbitrary", "arbitrary"),
        name="in_projection",
    )(x, g, sc, sh, w)


def _rwkv_kernel(pa_ref, mu_ref, w0_ref, w2_ref, a0_ref, a2_ref, g2_ref, kk_ref, ka_ref,
                 rk_ref, lw_ref, lb_ref, o_ref, carry_ref, s_ref, *, rows, tile, chunk):
    aw = A_WIDTH
    hd = A_HEAD_DIM
    n_chunks = tile // chunk

    @pl.when(pl.program_id(1) == 0)
    def _():
        carry_ref[...] = jnp.zeros_like(carry_ref)
        s_ref[...] = jnp.zeros_like(s_ref)

    ri = lax.broadcasted_iota(jnp.int32, (aw, aw), 0)
    ci = lax.broadcasted_iota(jnp.int32, (aw, aw), 1)
    same_head = (ri // hd) == (ci // hd)
    head_ones = same_head.astype(F32).astype(BF16)
    eye_full = ri == ci
    lane_head = lax.broadcasted_iota(jnp.int32, (hd, aw), 1) // hd
    t_c = lax.broadcasted_iota(jnp.int32, (chunk, aw), 0)
    s_c = lax.broadcasted_iota(jnp.int32, (chunk, aw), 1) % chunk
    strict = s_c < t_c
    incl = s_c <= t_c
    eye_c = (s_c == t_c).astype(F32)
    ti = lax.broadcasted_iota(jnp.int32, (tile, tile), 0)
    si = lax.broadcasted_iota(jnp.int32, (tile, tile), 1)
    tri = ((si <= ti) & ((si // chunk) == (ti // chunk))).astype(F32).astype(BF16)
    row = lax.broadcasted_iota(jnp.int32, (tile, 1), 0)

    def head_sum(x):
        return _split_dot(x, head_ones, 2)

    def blockdiag(x):
        return jnp.where(same_head, jnp.concatenate([x] * A_HEADS, axis=0), 0.0)

    seq = []
    for bi in range(rows):
        pa = pa_ref[bi]
        crow = bi * V7X_SUBLANES
        prev = jnp.where(row == 0, carry_ref[crow:crow + 1, :], pltpu.roll(pa, 1, 0))
        carry_ref[crow:crow + 1, :] = pa[tile - 1:tile, :]
        xs = pa + mu_ref[...] * (prev - pa)
        r = xs[:, 0:aw]
        k = xs[:, aw:2 * aw]
        v = xs[:, 2 * aw:3 * aw]
        lora = xs[:, 3 * aw:3 * aw + DECAY_LORA + ICLR_LORA]
        dg = xs[:, 3 * aw + DECAY_LORA + ICLR_LORA:]
        u = w0_ref[...] + _dot(jnp.tanh(lora), w2_ref[...])
        logdecay = -EXP_NEG_HALF * _sigmoid(u)
        iclr = _sigmoid(a0_ref[...] + _dot(lora, a2_ref[...]))
        gate = _dot(_sigmoid(dg), g2_ref[...])
        kk = k * kk_ref[...]
        kk = kk / jnp.maximum(jnp.sqrt(head_sum(kk * kk)), 1e-12)
        k = k * (1.0 + (iclr - 1.0) * ka_ref[...])
        b = kk * iclr
        cum = None
        rem = logdecay
        for _ in range(3):
            piece = rem.astype(BF16)
            term = jnp.dot(tri, piece, preferred_element_type=F32)
            cum = term if cum is None else cum + term
            rem = rem - piece.astype(F32)
        seq.append(dict(r=r, k=k, v=v, kk=kk, b=b, ld=logdecay, cum=cum, gate=gate))

    chains = [(bi, c) for bi in range(rows) for c in range(n_chunks)]
    ch = []
    for bi, c in chains:
        q = seq[bi]
        sl = slice(c * chunk, (c + 1) * chunk)
        cum_c = q["cum"][sl]
        total = cum_c[chunk - 1:chunk, :]
        w_inv = jnp.exp(-cum_c)
        w_end = jnp.exp(total - cum_c)
        ch.append(dict(
            rt=q["r"][sl] * jnp.exp(cum_c), at=-q["kk"][sl] * jnp.exp(cum_c - q["ld"][sl]),
            bt=q["b"][sl] * w_inv, kt=q["k"][sl] * w_inv, be=q["b"][sl] * w_end,
            ke=q["k"][sl] * w_end, vc=q["v"][sl], total=total))
    for d in ch:
        lhs = jnp.concatenate([d["at"], d["rt"]], axis=0)
        rhs = jnp.concatenate([blockdiag(d["bt"]), blockdiag(d["kt"])], axis=0)
        prod = _dot_nt(lhs, rhs)
        d["low"] = jnp.where(strict, prod[:chunk, :aw], 0.0)
        d["a_ak"] = jnp.where(strict, prod[:chunk, aw:], 0.0)
        d["a_rb"] = jnp.where(incl, prod[chunk:, :aw], 0.0)
        d["a_rk"] = jnp.where(incl, prod[chunk:, aw:], 0.0)
    for d in ch:
        d["pw"] = _dot(d["low"], blockdiag(d["low"]))
        d["ti"] = eye_c + d["low"]
        d["akv"] = _dot(d["a_ak"], blockdiag(d["vc"]))
        d["y_in"] = _dot(d["a_rk"], blockdiag(d["vc"]))
    for _ in range(int(math.log2(chunk)) - 2):
        for d in ch:
            both = _dot(jnp.concatenate([d["pw"], d["ti"]], axis=0), blockdiag(d["pw"]))
            d["pw"] = both[:chunk]
            d["ti"] = d["ti"] + both[chunk:]
    for d in ch:
        d["ti"] = d["ti"] + _dot(d["ti"], blockdiag(d["pw"]))
    for d in ch:
        both = _dot(d["ti"], jnp.concatenate([blockdiag(d["at"]), blockdiag(d["akv"])], axis=1))
        d["ah"] = both[:, :aw]
        d["vh"] = both[:, aw:]
    for d in ch:
        both = _dot(d["a_rb"], jnp.concatenate([blockdiag(d["ah"]), blockdiag(d["vh"])], axis=1))
        d["rh"] = d["rt"] + both[:, :aw]
        d["y_in"] = d["y_in"] + both[:, aw:]
        trans = jnp.where(same_head, _dot_tn(d["ah"], d["be"]), 0.0)
        d["trans"] = trans + jnp.where(eye_full, jnp.exp(d["total"]), 0.0)
        add_full = _dot_tn(jnp.concatenate([d["vh"], d["vc"]], axis=0),
                           jnp.concatenate([d["be"], d["ke"]], axis=0))
        add = None
        for h in range(A_HEADS):
            blk = jnp.where(lane_head == h, add_full[h * hd:(h + 1) * hd], 0.0)
            add = blk if add is None else add + blk
        d["add"] = add

    states = [s_ref[bi * hd:(bi + 1) * hd, :] for bi in range(rows)]
    ys = [[] for _ in range(rows)]
    for c in range(n_chunks):
        for bi in range(rows):
            d = ch[bi * n_chunks + c]
            ys[bi].append(_dot_nt(d["rh"], blockdiag(states[bi])) + d["y_in"])
            states[bi] = _dot(states[bi], d["trans"]) + d["add"]

    inv_n = 1.0 / hd
    for bi in range(rows):
        q = seq[bi]
        s_ref[bi * hd:(bi + 1) * hd, :] = states[bi]
        y = jnp.concatenate(ys[bi], axis=0)
        mu_h = head_sum(y) * inv_n
        yc = y - mu_h
        var_h = head_sum(yc * yc) * inv_n
        yn = yc * lax.rsqrt(var_h + LNX_EPS) * lw_ref[...] + lb_ref[...]
        bonus = head_sum(q["r"] * q["k"] * rk_ref[...]) * q["v"]
        o_ref[bi] = ((yn + bonus) * q["gate"]).astype(o_ref.dtype)


def _rwkv_mix(proj, mu, w0, w2p, a0, a2p, g2, k_k, k_a, r_k, lnx_w, lnx_b, tile, rows):
    bsz, t, _ = proj.shape
    vec = lambda n: pl.BlockSpec((1, n), lambda b, i: (0, 0))
    mat = lambda m, n: pl.BlockSpec((m, n), lambda b, i: (0, 0))
    return pl.pallas_call(
        functools.partial(_rwkv_kernel, rows=rows, tile=tile, chunk=RWKV_CHUNK),
        grid=(bsz // rows, t // tile),
        in_specs=[
            pl.BlockSpec((rows, tile, A_COLS), lambda b, i: (b, i, 0)),
            vec(A_COLS), vec(A_WIDTH), mat(DECAY_LORA + ICLR_LORA, A_WIDTH),
            vec(A_WIDTH), mat(DECAY_LORA + ICLR_LORA, A_WIDTH), mat(GATE_LORA, A_WIDTH),
            vec(A_WIDTH), vec(A_WIDTH), vec(A_WIDTH), vec(A_WIDTH), vec(A_WIDTH),
        ],
        out_specs=pl.BlockSpec((rows, tile, A_WIDTH), lambda b, i: (b, i, 0)),
        out_shape=jax.ShapeDtypeStruct((bsz, t, A_WIDTH), BF16),
        scratch_shapes=[pltpu.VMEM((rows * V7X_SUBLANES, A_COLS), F32),
                        pltpu.VMEM((rows * A_HEAD_DIM, A_WIDTH), F32)],
        compiler_params=_params("arbitrary", "arbitrary"),
        name="rwkv7_mix",
    )(proj, mu, w0, w2p, a0, a2p, g2, k_k, k_a, r_k, lnx_w, lnx_b)


LOG2E = math.log2(math.e)
ALIBI_PIECES = 3


def _half_rms(x, gain, scale):
    ri = lax.broadcasted_iota(jnp.int32, (x.shape[1], x.shape[1]), 0)
    ci = lax.broadcasted_iota(jnp.int32, (x.shape[1], x.shape[1]), 1)
    half_ones = ((ri // B_QK_DIM) == (ci // B_QK_DIM)).astype(F32).astype(BF16)
    ms = _split_dot(x * x, half_ones, 1) * (1.0 / B_QK_DIM)
    return x * lax.rsqrt(ms + RMS_EPS) * (gain * scale)


def _lane_fold(x, op):
    parts = [x[:, k * V7X_LANES:(k + 1) * V7X_LANES] for k in range(x.shape[1] // V7X_LANES)]
    while len(parts) > 1:
        pairs = [op(parts[a], parts[a + 1]) for a in range(0, len(parts) - 1, 2)]
        parts = pairs + parts[len(parts) - len(parts) % 2:]
    return parts[0]


def _attn_kernel(q_ref, k_ref, v_ref, qg_ref, kg_ref, sg_ref, lq1_ref, lk1_ref, lq2_ref,
                 lk2_ref, slope_ref, bias_ref, o_ref, kx_ref, vb_ref, corr_ref, *, tq, heads,
                 lambda_init):
    i = pl.program_id(2)
    t_all = k_ref.shape[1]
    half = tq // 2
    hw = B_V_DIM
    hsl = [slice(h * hw, (h + 1) * hw) for h in range(heads)]

    @pl.when(i == 0)
    def _():
        r = lax.broadcasted_iota(jnp.int32, (half, half), 0)
        c = lax.broadcasted_iota(jnp.int32, (half, half), 1)
        for h in range(heads):
            slope2 = slope_ref[h][:, 0:1] * LOG2E
            kx_ref[h, :, 0:hw] = _half_rms(k_ref[0, :, hsl[h]], kg_ref[...], 1.0).astype(BF16)
            kx_ref[h, :, hw:] = bias_ref[h]
            vb_ref[h] = v_ref[0, :, hsl[h]].astype(BF16)
            after = jnp.where(c > r, (c - r).astype(F32) * (-2.0 * slope2), 0.0)
            corr_ref[h] = jnp.where((c // CHUNK) <= (r // CHUNK), after, NEG_INF)

    lane = lax.broadcasted_iota(jnp.int32, (tq, V7X_LANES), 1)
    lo = lane < B_QK_DIM
    ones = (lane < ALIBI_PIECES).astype(F32)
    qxs = []
    for h in range(heads):
        qn = _half_rms(q_ref[0, :, hsl[h]], qg_ref[...], LOG2E * B_QK_DIM ** -0.5)
        q0 = jnp.concatenate([jnp.where(lo, qn, 0.0), ones], axis=1)
        q1 = jnp.concatenate([jnp.where(lo, 0.0, qn), ones], axis=1)
        qxs.append(jnp.concatenate([q0[:half], q1[:half], q0[half:], q1[half:]],
                                   axis=0).astype(BF16))

    lam = (jnp.exp(jnp.sum(lq1_ref[...] * lk1_ref[...], axis=-1, keepdims=True))
           - jnp.exp(jnp.sum(lq2_ref[...] * lk2_ref[...], axis=-1, keepdims=True))
           + lambda_init)

    def scores(h, lhs, start, size):
        return lax.dot_general(lhs, kx_ref[h, start:start + size, :], (((1,), (1,)), ((), ())),
                               preferred_element_type=F32)

    def update(h, s, carry, start, size):
        m, l, acc = carry
        m_new = jnp.maximum(m, jnp.max(s, axis=-1, keepdims=True))
        alpha = jnp.exp2(m - m_new)
        p = jnp.exp2(s - m_new)
        l = alpha * l + jnp.sum(p, axis=-1, keepdims=True)
        pv = jnp.dot(p.astype(BF16), vb_ref[h, start:start + size, :], preferred_element_type=F32)
        return m_new, l, alpha * acc + pv

    def attend(n):
        d0 = n * tq
        carry = [(jnp.full((2 * tq, 1), NEG_INF, F32), jnp.zeros((2 * tq, 1), F32),
                  jnp.zeros((2 * tq, B_V_DIM), F32)) for _ in range(heads)]
        for j in range(n):
            for h in range(heads):
                carry[h] = update(h, scores(h, qxs[h], j * tq, tq), carry[h], j * tq, tq)
        corr2 = [jnp.concatenate([corr_ref[h], corr_ref[h]], axis=0) for h in range(heads)]
        for h in range(heads):
            s1 = scores(h, qxs[h], d0, half)
            s1 = jnp.concatenate([s1[:tq] + corr2[h], s1[tq:]], axis=0)
            carry[h] = update(h, s1, carry[h], d0, half)
        outs = []
        for h in range(heads):
            m, l, acc = carry[h]
            s2 = scores(h, qxs[h][tq:], d0 + half, half) + corr2[h]
            _, l_b, acc_b = update(h, s2, (m[tq:], l[tq:], acc[tq:]), d0 + half, half)
            o_a = acc[:tq] / l[:tq]
            o_b = acc_b / l_b
            d = jnp.concatenate([o_a[:half] - lam * o_a[half:], o_b[:half] - lam * o_b[half:]],
                                axis=0)
            ms = jnp.mean(d * d, axis=-1, keepdims=True)
            outs.append(d * lax.rsqrt(ms + RMS_EPS) * sg_ref[...] * (1.0 - lambda_init))
        o_ref[0] = jnp.concatenate(outs, axis=1).astype(o_ref.dtype)

    for n in range(t_all // tq):
        pl.when(i == n)(functools.partial(attend, n))


def _diff_attention(proj, qg2, kg2, sg, lq1, lk1, lq2, lk2, slopes, lambda_init, tq, heads):
    bsz, t, _ = proj.shape
    width = heads * B_V_DIM
    q_blk = A_COLS // width
    k_blk = (A_COLS + B_QK_COLS) // width
    v_blk = (A_COLS + 2 * B_QK_COLS) // width
    vec = lambda n: pl.BlockSpec((1, n), lambda b, h, i: (0, 0))
    return pl.pallas_call(
        functools.partial(_attn_kernel, tq=tq, heads=heads, lambda_init=lambda_init),
        grid=(bsz, B_HEADS // heads, t // tq),
        in_specs=[
            pl.BlockSpec((1, tq, width), lambda b, h, i: (b, i, q_blk + h)),
            pl.BlockSpec((1, t, width), lambda b, h, i: (b, 0, k_blk + h)),
            pl.BlockSpec((1, t, width), lambda b, h, i: (b, 0, v_blk + h)),
            vec(B_V_DIM), vec(B_V_DIM), vec(B_V_DIM),
            vec(B_QK_DIM), vec(B_QK_DIM), vec(B_QK_DIM), vec(B_QK_DIM),
            pl.BlockSpec((heads, 1, V7X_LANES), lambda b, h, i: (h, 0, 0)),
            pl.BlockSpec((heads, t, V7X_LANES), lambda b, h, i: (h, 0, 0)),
        ],
        out_specs=pl.BlockSpec((1, tq, width), lambda b, h, i: (b, i, h)),
        out_shape=jax.ShapeDtypeStruct((bsz, t, B_WIDTH), BF16),
        scratch_shapes=[pltpu.VMEM((heads, t, B_V_DIM + V7X_LANES), BF16),
                        pltpu.VMEM((heads, t, B_V_DIM), BF16),
                        pltpu.VMEM((heads, tq // 2, tq // 2), F32)],
        compiler_params=_params("arbitrary", "arbitrary", "arbitrary"),
        name="diff_attention",
    )(proj, proj, proj, qg2, kg2, sg, lq1, lk1, lq2, lk2, slopes, _alibi_key_bias(slopes, t))


def _conv_kernel(pc_ref, w_ref, b_ref, lw_ref, lb_ref, o_ref, hbuf_ref, *, tile, rows):
    @pl.when(pl.program_id(1) == 0)
    def _():
        hbuf_ref[0, 0:CONV_HALO, :] = jnp.zeros((CONV_HALO, C_WIDTH), F32)

    pc = pc_ref[0]
    n = CONV_HALO + tile
    hbuf_ref[0, CONV_HALO:n, :] = pc[:, :C_WIDTH] * _sigmoid(pc[:, C_WIDTH:])
    full = hbuf_ref[0]
    for s in range(1, V7X_SUBLANES):
        hbuf_ref[s] = pltpu.roll(full, n - s, 0)
    lead = CONV_HALO - (CONV_WIDTH - 1)
    for c in range(tile // rows):
        acc = jnp.zeros((rows, C_WIDTH), F32)
        for j in range(CONV_WIDTH):
            start = c * rows + lead + j
            s = start % V7X_SUBLANES
            acc = acc + w_ref[j:j + 1, :] * hbuf_ref[s, start - s:start - s + rows, :]
        acc = acc + b_ref[...]
        mu = jnp.mean(acc, axis=-1, keepdims=True)
        d = acc - mu
        var = jnp.mean(d * d, axis=-1, keepdims=True)
        y = d * lax.rsqrt(var + LN_EPS) * lw_ref[...] + lb_ref[...]
        o_ref[0, c * rows:(c + 1) * rows, :] = (y * _sigmoid(y)).astype(o_ref.dtype)
    hbuf_ref[0, 0:CONV_HALO, :] = hbuf_ref[0, tile:n, :]


def _conformer_conv(proj, conv_w, conv_b, ln_w, ln_b, tile):
    bsz, t, _ = proj.shape
    c_blk = (A_COLS + B_COLS) // C_COLS
    vec = lambda n: pl.BlockSpec((1, n), lambda b, i: (0, 0))
    return pl.pallas_call(
        functools.partial(_conv_kernel, tile=tile, rows=64),
        grid=(bsz, t // tile),
        in_specs=[
            pl.BlockSpec((1, tile, C_COLS), lambda b, i: (b, i, c_blk)),
            pl.BlockSpec((CONV_WIDTH, C_WIDTH), lambda b, i: (0, 0)),
            vec(C_WIDTH), vec(C_WIDTH), vec(C_WIDTH),
        ],
        out_specs=pl.BlockSpec((1, tile, C_WIDTH), lambda b, i: (b, i, 0)),
        out_shape=jax.ShapeDtypeStruct((bsz, t, C_WIDTH), BF16),
        scratch_shapes=[pltpu.VMEM((V7X_SUBLANES, CONV_HALO + tile, C_WIDTH), F32)],
        compiler_params=_params("arbitrary", "arbitrary"),
        name="conformer_conv",
    )(proj, conv_w, conv_b, ln_w, ln_b)


def _mix_mlp_kernel(x_ref, ya_ref, yb_ref, yc_ref, wo_ref, g1_ref, g_ref, sc_ref, sh_ref, g2_ref,
                    w1_ref, w2_ref, o_ref, xn_ref, h_ref, acc_ref):
    f = pl.program_id(2)

    @pl.when(f == 0)
    def _():
        y = jnp.dot(ya_ref[0], wo_ref[0:A_WIDTH, :], preferred_element_type=F32)
        y = y + jnp.dot(yb_ref[0], wo_ref[A_WIDTH:A_WIDTH + B_WIDTH, :],
                        preferred_element_type=F32)
        y = y + jnp.dot(yc_ref[0], wo_ref[A_WIDTH + B_WIDTH:, :], preferred_element_type=F32)
        xn = x_ref[0] + g1_ref[0] * y
        xn_ref[...] = xn
        h_ref[...] = _norm_mod(xn, g_ref[...], sc_ref[0], sh_ref[0]).astype(BF16)
        acc_ref[...] = jnp.zeros_like(acc_ref)

    a = jnp.dot(h_ref[...], w1_ref[...], preferred_element_type=F32)
    a = jnp.maximum(a, 0.0)
    acc_ref[...] += jnp.dot((a * a).astype(BF16), w2_ref[...], preferred_element_type=F32)

    @pl.when(f == pl.num_programs(2) - 1)
    def _():
        o_ref[0] = xn_ref[...] + g2_ref[0] * acc_ref[...]


def _mix_mlp(x, ya, yb, yc, wo, g1, g, sc, sh, g2, w1, w2, tm, tf):
    bsz, t, d = x.shape
    dff = w1.shape[1]
    row = pl.BlockSpec((1, 1, d), lambda b, i, f: (b, 0, 0))
    blk = lambda n: pl.BlockSpec((1, tm, n), lambda b, i, f: (b, i, 0))
    return pl.pallas_call(
        _mix_mlp_kernel,
        grid=(bsz, t // tm, dff // tf),
        in_specs=[
            blk(d), blk(A_WIDTH), blk(B_WIDTH), blk(C_WIDTH),
            pl.BlockSpec((d, d), lambda b, i, f: (0, 0)),
            row,
            pl.BlockSpec((1, d), lambda b, i, f: (0, 0)),
            row, row, row,
            pl.BlockSpec((d, tf), lambda b, i, f: (0, f)),
            pl.BlockSpec((tf, d), lambda b, i, f: (f, 0)),
        ],
        out_specs=blk(d),
        out_shape=jax.ShapeDtypeStruct((bsz, t, d), F32),
        scratch_shapes=[pltpu.VMEM((tm, d), F32), pltpu.VMEM((tm, d), BF16),
                        pltpu.VMEM((tm, d), F32)],
        compiler_params=_params("arbitrary", "arbitrary", "arbitrary"),
        name="mix_mlp",
    )(x, ya, yb, yc, wo, g1, g, sc, sh, g2, w1, w2)


def _alibi_slopes():
    s = jnp.asarray([2.0 ** (-8.0 * (h + 1) / B_HEADS) for h in range(B_HEADS)], F32)
    return jnp.broadcast_to(s[:, None, None], (B_HEADS, 1, V7X_LANES))


def _alibi_key_bias(slopes, t):
    rem = jnp.arange(t, dtype=F32)[None, :, None] * (slopes[:, :, 0:1] * LOG2E)
    pieces = []
    for _ in range(ALIBI_PIECES):
        bits = lax.bitcast_convert_type(rem, jnp.uint32) & jnp.uint32(0xFFFF0000)
        part = lax.bitcast_convert_type(bits, F32)
        pieces.append(part.astype(BF16))
        rem = rem - part
    table = jnp.concatenate(pieces, axis=-1)
    return jnp.pad(table, ((0, 0), (0, 0), (0, V7X_LANES - ALIBI_PIECES)))


def _pad_rows(w, before, after):
    return jnp.pad(w, ((before, after), (0, 0)))


def _layer(x, mod, p, lambda_init, tiles):
    bsz = x.shape[0]
    sh1, sc1, g1, sh2, sc2, g2 = [m.reshape(bsz, 1, D_MODEL) for m in jnp.split(mod, 6, axis=-1)]
    row = lambda a: a.reshape(1, -1)

    proj = _in_projection(x, row(p["norm1_g"]), sc1, sh1, p["w_in"].astype(BF16), tiles["proj"])
    y_a = _rwkv_mix(
        proj, row(p["tshift_mu"]), row(p["decay_w0"]),
        _pad_rows(p["decay_w2"], 0, ICLR_LORA).astype(BF16), row(p["iclr_a0"]),
        _pad_rows(p["iclr_a2"], DECAY_LORA, 0).astype(BF16), p["gate_g2"].astype(BF16),
        row(p["k_k"]), row(p["k_a"]), row(p["r_k"]), row(p["lnx_w"]), row(p["lnx_b"]),
        tiles["rwkv"], RWKV_ROWS)
    two = lambda a: row(jnp.concatenate([a, a]))
    y_b = _diff_attention(
        proj, two(p["q_norm_g"]), two(p["k_norm_g"]), row(p["subln_g"]),
        row(p["lambda_q1"]), row(p["lambda_k1"]), row(p["lambda_q2"]), row(p["lambda_k2"]),
        _alibi_slopes(), lambda_init, tiles["attn"], ATTN_HEADS)
    y_c = _conformer_conv(proj, p["conv_w"], row(p["conv_b"]), row(p["conv_ln_w"]),
                          row(p["conv_ln_b"]), tiles["conv"])
    return _mix_mlp(x, y_a, y_b, y_c, p["w_out"].astype(BF16), g1, row(p["norm2_g"]), sc2, sh2,
                    g2, p["mlp_w1"].astype(BF16), p["mlp_w2"].astype(BF16), tiles["mlp_tm"],
                    tiles["mlp_tf"])


_LAYER_PARAMS = ("norm1_g", "w_in", "tshift_mu", "decay_w0", "decay_w2", "iclr_a0", "iclr_a2",
                 "gate_g2", "k_k", "k_a", "r_k", "lnx_w", "lnx_b", "q_norm_g", "k_norm_g",
                 "lambda_q1", "lambda_k1", "lambda_q2", "lambda_k2", "subln_g", "conv_w",
                 "conv_b", "conv_ln_w", "conv_ln_b", "w_out", "norm2_g", "mlp_w1", "mlp_w2")


def _block(x, c, ada_w, ada_b, params, tiles):
    mod = _ada_modulation(c, ada_w, ada_b)
    for i in range(ada_w.shape[0]):
        lambda_init = 0.8 - 0.6 * math.exp(-0.3 * i)
        x = _layer(x, mod[i], {k: v[i] for k, v in params.items()}, lambda_init, tiles)
    return x


def kernel(x, c, ada_w, ada_b, norm1_g, w_in, tshift_mu, decay_w0, decay_w2, iclr_a0, iclr_a2,
           gate_g2, k_k, k_a, r_k, lnx_w, lnx_b, q_norm_g, k_norm_g, lambda_q1, lambda_k1,
           lambda_q2, lambda_k2, subln_g, conv_w, conv_b, conv_ln_w, conv_ln_b, w_out, norm2_g,
           mlp_w1, mlp_w2):
    values = (norm1_g, w_in, tshift_mu, decay_w0, decay_w2, iclr_a0, iclr_a2, gate_g2, k_k, k_a,
              r_k.reshape(r_k.shape[0], -1), lnx_w, lnx_b, q_norm_g, k_norm_g, lambda_q1,
              lambda_k1, lambda_q2, lambda_k2, subln_g, conv_w, conv_b, conv_ln_w, conv_ln_b,
              w_out, norm2_g, mlp_w1, mlp_w2)
    t = x.shape[1]
    tiles = dict(proj=min(PROJ_TM, t), rwkv=min(RWKV_TILE, t), conv=min(CONV_TILE, t),
                 attn=min(ATTN_TQ, t),
                 mlp_tm=min(MLP_TM, t), mlp_tf=MLP_TF)
    return _block(x, c, ada_w, ada_b, dict(zip(_LAYER_PARAMS, values)), tiles)
```

```python
import functools
import math

import jax
import jax.numpy as jnp
from jax import lax
from jax.experimental import pallas as pl
from jax.experimental.pallas import tpu as pltpu

F32 = jnp.float32
BF16 = jnp.bfloat16

D_MODEL = 1024
DEPTH = 2
CHUNK = 64
A_HEADS = 4
A_HEAD_DIM = 64
A_WIDTH = A_HEADS * A_HEAD_DIM
DECAY_LORA = 64
ICLR_LORA = 64
GATE_LORA = 128
A_COLS = 3 * A_WIDTH + DECAY_LORA + ICLR_LORA + GATE_LORA
B_HEADS = 4
B_QK_DIM = 64
B_V_DIM = 2 * B_QK_DIM
B_WIDTH = B_HEADS * B_V_DIM
B_QK_COLS = B_HEADS * 2 * B_QK_DIM
B_COLS = 2 * B_QK_COLS + B_WIDTH
C_WIDTH = D_MODEL - A_WIDTH - B_WIDTH
CONV_WIDTH = 31
C_COLS = 2 * C_WIDTH
N_IN = A_COLS + B_COLS + C_COLS
D_FF = 4 * D_MODEL
RMS_EPS = 1e-6
LN_EPS = 1e-5
LNX_EPS = 64e-5
NEG_INF = -1e30
EXP_NEG_HALF = math.exp(-0.5)

V7X_LANES = 128
V7X_SUBLANES = 8
V7X_VMEM_LIMIT_BYTES = 56 * 1024 * 1024

RWKV_CHUNK = 64
RWKV_TILE = 256
RWKV_ROWS = 4
ATTN_TQ = 512
ATTN_HEADS = 2
CONV_TILE = 512
CONV_HALO = 32
PROJ_TM = 1024
MLP_TM = 512
MLP_TF = 2048


def _params(*sem):
    return pltpu.CompilerParams(dimension_semantics=sem,
                                vmem_limit_bytes=V7X_VMEM_LIMIT_BYTES)


def _dot(a, b):
    return jnp.dot(a.astype(BF16), b.astype(BF16), preferred_element_type=F32)


def _dot_nt(a, b):
    return lax.dot_general(a.astype(BF16), b.astype(BF16), (((1,), (1,)), ((), ())),
                           preferred_element_type=F32)


def _dot_tn(a, b):
    return lax.dot_general(a.astype(BF16), b.astype(BF16), (((0,), (0,)), ((), ())),
                           preferred_element_type=F32)


def _split_dot(a, b_exact, parts):
    out = None
    rem = a
    for _ in range(parts):
        piece = rem.astype(BF16)
        term = jnp.dot(piece, b_exact, preferred_element_type=F32)
        out = term if out is None else out + term
        rem = rem - piece.astype(F32)
    return out


def _sigmoid(x):
    return 1.0 / (1.0 + jnp.exp(-x))


def _ada_kernel(c_ref, w_ref, b_ref, o_ref):
    c = c_ref[...]
    cond = c * _sigmoid(c)
    o_ref[0] = _dot(cond, w_ref[0]) + b_ref[0]


def _ada_modulation(c, ada_w, ada_b):
    depth, d, n = ada_w.shape
    bsz = c.shape[0]
    tn = 1536
    return pl.pallas_call(
        _ada_kernel,
        grid=(depth, n // tn),
        in_specs=[
            pl.BlockSpec((bsz, d), lambda l, j: (0, 0)),
            pl.BlockSpec((1, d, tn), lambda l, j: (l, 0, j)),
            pl.BlockSpec((1, 1, tn), lambda l, j: (l, 0, j)),
        ],
        out_specs=pl.BlockSpec((1, bsz, tn), lambda l, j: (l, 0, j)),
        out_shape=jax.ShapeDtypeStruct((depth, bsz, n), F32),
        compiler_params=_params("arbitrary", "arbitrary"),
        name="ada_modulation",
    )(c, ada_w, ada_b.reshape(depth, 1, n))


def _norm_mod(x, g, sc, sh):
    ms = jnp.mean(x * x, axis=-1, keepdims=True)
    y = x * lax.rsqrt(ms + RMS_EPS) * g
    return y * (1.0 + sc) + sh


def _inproj_kernel(x_ref, g_ref, sc_ref, sh_ref, w_ref, o_ref, *, tn):
    h = _norm_mod(x_ref[0], g_ref[...], sc_ref[0], sh_ref[0]).astype(BF16)
    n = w_ref.shape[1]
    for j in range(n // tn):
        o_ref[0, :, j * tn:(j + 1) * tn] = jnp.dot(
            h, w_ref[:, j * tn:(j + 1) * tn], preferred_element_type=F32)


def _in_projection(x, g, sc, sh, w, tm):
    bsz, t, d = x.shape
    n = w.shape[1]
    return pl.pallas_call(
        functools.partial(_inproj_kernel, tn=1024),
        grid=(bsz, t // tm),
        in_specs=[
            pl.BlockSpec((1, tm, d), lambda b, i: (b, i, 0)),
            pl.BlockSpec((1, d), lambda b, i: (0, 0)),
            pl.BlockSpec((1, 1, d), lambda b, i: (b, 0, 0)),
            pl.BlockSpec((1, 1, d), lambda b, i: (b, 0, 0)),
            pl.BlockSpec((d, n), lambda b, i: (0, 0)),
        ],
        out_specs=pl.BlockSpec((1, tm, n), lambda b, i: (b, i, 0)),
        out_shape=jax.ShapeDtypeStruct((bsz, t, n), F32),
        compiler_params=_params("arbitrary", "arbitrary"),
        name="in_projection",
    )(x, g, sc, sh, w)


def _rwkv_kernel(pa_ref, mu_ref, w0_ref, w2_ref, a0_ref, a2_ref, g2_ref, kk_ref, ka_ref,
                 rk_ref, lw_ref, lb_ref, ho_ref, tri_ref, o_ref, carry_ref, s_ref, *, rows, tile,
                 chunk):
    aw = A_WIDTH
    hd = A_HEAD_DIM
    n_chunks = tile // chunk

    @pl.when(pl.program_id(1) == 0)
    def _():
        carry_ref[...] = jnp.zeros_like(carry_ref)
        s_ref[...] = jnp.zeros_like(s_ref)

    ri = lax.broadcasted_iota(jnp.int32, (aw, aw), 0)
    ci = lax.broadcasted_iota(jnp.int32, (aw, aw), 1)
    same_head = (ri // hd) == (ci // hd)
    head_ones = ho_ref[...]
    eye_full = ri == ci
    lane_head = lax.broadcasted_iota(jnp.int32, (hd, aw), 1) // hd
    t_c = lax.broadcasted_iota(jnp.int32, (chunk, aw), 0)
    s_c = lax.broadcasted_iota(jnp.int32, (chunk, aw), 1) % chunk
    strict = s_c < t_c
    incl = s_c <= t_c
    eye_c = (s_c == t_c).astype(F32)
    tri = tri_ref[...]
    row = lax.broadcasted_iota(jnp.int32, (tile, 1), 0)

    def head_sum(x, parts=1):
        return _split_dot(x, head_ones, parts)

    def blockdiag(x):
        return jnp.where(same_head, jnp.concatenate([x] * A_HEADS, axis=0), 0.0)

    seq = []
    for bi in range(rows):
        pa = pa_ref[bi]
        crow = bi * V7X_SUBLANES
        prev = jnp.where(row == 0, carry_ref[crow:crow + 1, :], pltpu.roll(pa, 1, 0))
        carry_ref[crow:crow + 1, :] = pa[tile - 1:tile, :]
        xs = pa + mu_ref[...] * (prev - pa)
        r = xs[:, 0:aw]
        k = xs[:, aw:2 * aw]
        v = xs[:, 2 * aw:3 * aw]
        lora = xs[:, 3 * aw:3 * aw + DECAY_LORA + ICLR_LORA]
        dg = xs[:, 3 * aw + DECAY_LORA + ICLR_LORA:]
        u = w0_ref[...] + _dot(jnp.tanh(lora), w2_ref[...])
        logdecay = -EXP_NEG_HALF * _sigmoid(u)
        iclr = _sigmoid(a0_ref[...] + _dot(lora, a2_ref[...]))
        gate = _dot(_sigmoid(dg), g2_ref[...])
        kk = k * kk_ref[...]
        kk = kk * lax.rsqrt(jnp.maximum(head_sum(kk * kk, 2), 1e-24))
        k = k * (1.0 + (iclr - 1.0) * ka_ref[...])
        b = kk * iclr
        cum = None
        rem = logdecay
        for _ in range(3):
            piece = rem.astype(BF16)
            term = jnp.dot(tri, piece, preferred_element_type=F32)
            cum = term if cum is None else cum + term
            rem = rem - piece.astype(F32)
        seq.append(dict(r=r, k=k, v=v, kk=kk, b=b, ld=logdecay, cum=cum, gate=gate))

    chains = [(bi, c) for bi in range(rows) for c in range(n_chunks)]
    ch = []
    for bi, c in chains:
        q = seq[bi]
        sl = slice(c * chunk, (c + 1) * chunk)
        cum_c = q["cum"][sl]
        total = cum_c[chunk - 1:chunk, :]
        w_inv = jnp.exp(-cum_c)
        w_end = jnp.exp(total - cum_c)
        ch.append(dict(
            rt=q["r"][sl] * jnp.exp(cum_c), at=-q["kk"][sl] * jnp.exp(cum_c - q["ld"][sl]),
            bt=q["b"][sl] * w_inv, kt=q["k"][sl] * w_inv, be=q["b"][sl] * w_end,
            ke=q["k"][sl] * w_end, vc=q["v"][sl], total=total))
    for d in ch:
        lhs = jnp.concatenate([d["at"], d["rt"]], axis=0)
        rhs = jnp.concatenate([blockdiag(d["bt"]), blockdiag(d["kt"])], axis=0)
        prod = _dot_nt(lhs, rhs)
        d["low"] = jnp.where(strict, prod[:chunk, :aw], 0.0)
        d["a_ak"] = jnp.where(strict, prod[:chunk, aw:], 0.0)
        d["a_rb"] = jnp.where(incl, prod[chunk:, :aw], 0.0)
        d["a_rk"] = jnp.where(incl, prod[chunk:, aw:], 0.0)
    for d in ch:
        d["pw"] = _dot(d["low"], blockdiag(d["low"]))
        d["ti"] = eye_c + d["low"]
        d["akv"] = _dot(d["a_ak"], blockdiag(d["vc"]))
        d["y_in"] = _dot(d["a_rk"], blockdiag(d["vc"]))
    for _ in range(int(math.log2(chunk)) - 2):
        for d in ch:
            both = _dot(jnp.concatenate([d["pw"], d["ti"]], axis=0), blockdiag(d["pw"]))
            d["pw"] = both[:chunk]
            d["ti"] = d["ti"] + both[chunk:]
    for d in ch:
        d["ti"] = d["ti"] + _dot(d["ti"], blockdiag(d["pw"]))
    for d in ch:
        both = _dot(d["ti"], jnp.concatenate([blockdiag(d["at"]), blockdiag(d["akv"])], axis=1))
        d["ah"] = both[:, :aw]
        d["vh"] = both[:, aw:]
    for d in ch:
        both = _dot(d["a_rb"], jnp.concatenate([blockdiag(d["ah"]), blockdiag(d["vh"])], axis=1))
        d["rh"] = d["rt"] + both[:, :aw]
        d["y_in"] = d["y_in"] + both[:, aw:]
        trans = jnp.where(same_head, _dot_tn(d["ah"], d["be"]), 0.0)
        d["trans"] = trans + jnp.where(eye_full, jnp.exp(d["total"]), 0.0)
        add_full = _dot_tn(jnp.concatenate([d["vh"], d["vc"]], axis=0),
                           jnp.concatenate([d["be"], d["ke"]], axis=0))
        add = None
        for h in range(A_HEADS):
            blk = jnp.where(lane_head == h, add_full[h * hd:(h + 1) * hd], 0.0)
            add = blk if add is None else add + blk
        d["add"] = add

    states = [s_ref[bi * hd:(bi + 1) * hd, :] for bi in range(rows)]
    ys = [[] for _ in range(rows)]
    for c in range(n_chunks):
        for bi in range(rows):
            d = ch[bi * n_chunks + c]
            ys[bi].append(_dot_nt(d["rh"], blockdiag(states[bi])) + d["y_in"])
            states[bi] = _dot(states[bi], d["trans"]) + d["add"]

    inv_n = 1.0 / hd
    for bi in range(rows):
        q = seq[bi]
        s_ref[bi * hd:(bi + 1) * hd, :] = states[bi]
        y = jnp.concatenate(ys[bi], axis=0)
        mu_h = head_sum(y) * inv_n
        yc = y - mu_h
        var_h = head_sum(yc * yc) * inv_n
        yn = yc * lax.rsqrt(var_h + LNX_EPS) * lw_ref[...] + lb_ref[...]
        bonus = head_sum(q["r"] * q["k"] * rk_ref[...]) * q["v"]
        o_ref[bi] = ((yn + bonus) * q["gate"]).astype(o_ref.dtype)


def _rwkv_mix(proj, mu, w0, w2p, a0, a2p, g2, k_k, k_a, r_k, lnx_w, lnx_b, tile, rows):
    bsz, t, _ = proj.shape
    rows = math.gcd(rows, bsz)
    vec = lambda n: pl.BlockSpec((1, n), lambda b, i: (0, 0))
    mat = lambda m, n: pl.BlockSpec((m, n), lambda b, i: (0, 0))
    lane_head = jnp.arange(A_WIDTH) // A_HEAD_DIM
    head_ones = (lane_head[:, None] == lane_head[None, :]).astype(BF16)
    pos = jnp.arange(tile)
    tri = ((pos[None, :] <= pos[:, None])
           & (pos[None, :] // RWKV_CHUNK == pos[:, None] // RWKV_CHUNK)).astype(BF16)
    return pl.pallas_call(
        functools.partial(_rwkv_kernel, rows=rows, tile=tile, chunk=RWKV_CHUNK),
        grid=(bsz // rows, t // tile),
        in_specs=[
            pl.BlockSpec((rows, tile, A_COLS), lambda b, i: (b, i, 0)),
            vec(A_COLS), vec(A_WIDTH), mat(DECAY_LORA + ICLR_LORA, A_WIDTH),
            vec(A_WIDTH), mat(DECAY_LORA + ICLR_LORA, A_WIDTH), mat(GATE_LORA, A_WIDTH),
            vec(A_WIDTH), vec(A_WIDTH), vec(A_WIDTH), vec(A_WIDTH), vec(A_WIDTH),
            mat(A_WIDTH, A_WIDTH), mat(tile, tile),
        ],
        out_specs=pl.BlockSpec((rows, tile, A_WIDTH), lambda b, i: (b, i, 0)),
        out_shape=jax.ShapeDtypeStruct((bsz, t, A_WIDTH), BF16),
        scratch_shapes=[pltpu.VMEM((rows * V7X_SUBLANES, A_COLS), F32),
                        pltpu.VMEM((rows * A_HEAD_DIM, A_WIDTH), F32)],
        compiler_params=_params("arbitrary", "arbitrary"),
        name="rwkv7_mix",
    )(proj, mu, w0, w2p, a0, a2p, g2, k_k, k_a, r_k, lnx_w, lnx_b, head_ones, tri)


LOG2E = math.log2(math.e)
ALIBI_PIECES = 3


def _half_rms(x, gain, scale):
    ri = lax.broadcasted_iota(jnp.int32, (x.shape[1], x.shape[1]), 0)
    ci = lax.broadcasted_iota(jnp.int32, (x.shape[1], x.shape[1]), 1)
    half_ones = ((ri // B_QK_DIM) == (ci // B_QK_DIM)).astype(F32).astype(BF16)
    ms = _split_dot(x * x, half_ones, 1) * (1.0 / B_QK_DIM)
    return x * lax.rsqrt(ms + RMS_EPS) * (gain * scale)


def _lane_fold(x, op):
    parts = [x[:, k * V7X_LANES:(k + 1) * V7X_LANES] for k in range(x.shape[1] // V7X_LANES)]
    while len(parts) > 1:
        pairs = [op(parts[a], parts[a + 1]) for a in range(0, len(parts) - 1, 2)]
        parts = pairs + parts[len(parts) - len(parts) % 2:]
    return parts[0]


def _attn_kernel(q_ref, k_ref, v_ref, qg_ref, kg_ref, sg_ref, lq1_ref, lk1_ref, lq2_ref,
                 lk2_ref, slope_ref, bias_ref, o_ref, kx_ref, vb_ref, corr_ref, *, tq, heads,
                 lambda_init):
    i = pl.program_id(2)
    t_all = k_ref.shape[1]
    half = tq // 2
    hw = B_V_DIM
    hsl = [slice(h * hw, (h + 1) * hw) for h in range(heads)]

    @pl.when(i == 0)
    def _():
        r = lax.broadcasted_iota(jnp.int32, (half, half), 0)
        c = lax.broadcasted_iota(jnp.int32, (half, half), 1)
        for h in range(heads):
            slope2 = slope_ref[h][:, 0:1] * LOG2E
            kx_ref[h, :, 0:hw] = _half_rms(k_ref[0, :, hsl[h]], kg_ref[...], 1.0).astype(BF16)
            kx_ref[h, :, hw:] = bias_ref[h]
            vb_ref[h] = v_ref[0, :, hsl[h]].astype(BF16)
            after = jnp.where(c > r, (c - r).astype(F32) * (-2.0 * slope2), 0.0)
            corr_ref[h] = jnp.where((c // CHUNK) <= (r // CHUNK), after, NEG_INF)

    lane = lax.broadcasted_iota(jnp.int32, (tq, V7X_LANES), 1)
    lo = lane < B_QK_DIM
    ones = (lane < ALIBI_PIECES).astype(F32)
    qxs = []
    for h in range(heads):
        qn = _half_rms(q_ref[0, :, hsl[h]], qg_ref[...], LOG2E * B_QK_DIM ** -0.5)
        q0 = jnp.concatenate([jnp.where(lo, qn, 0.0), ones], axis=1)
        q1 = jnp.concatenate([jnp.where(lo, 0.0, qn), ones], axis=1)
        qxs.append(jnp.concatenate([q0[:half], q1[:half], q0[half:], q1[half:]],
                                   axis=0).astype(BF16))

    lam = (jnp.exp(jnp.sum(lq1_ref[...] * lk1_ref[...], axis=-1, keepdims=True))
           - jnp.exp(jnp.sum(lq2_ref[...] * lk2_ref[...], axis=-1, keepdims=True))
           + lambda_init)

    def scores(h, lhs, start, size):
        return lax.dot_general(lhs, kx_ref[h, start:start + size, :], (((1,), (1,)), ((), ())),
                               preferred_element_type=F32)

    def update(h, s, carry, start, size):
        m, l, acc = carry
        m_new = jnp.maximum(m, jnp.max(s, axis=-1, keepdims=True))
        alpha = jnp.exp2(m - m_new)
        p = jnp.exp2(s - m_new)
        l = alpha * l + jnp.sum(p, axis=-1, keepdims=True)
        pv = jnp.dot(p.astype(BF16), vb_ref[h, start:start + size, :], preferred_element_type=F32)
        return m_new, l, alpha * acc + pv

    def attend(n):
        d0 = n * tq
        carry = [(jnp.full((2 * tq, 1), NEG_INF, F32), jnp.zeros((2 * tq, 1), F32),
                  jnp.zeros((2 * tq, B_V_DIM), F32)) for _ in range(heads)]
        for j in range(n):
            for h in range(heads):
                carry[h] = update(h, scores(h, qxs[h], j * tq, tq), carry[h], j * tq, tq)
        corr2 = [jnp.concatenate([corr_ref[h], corr_ref[h]], axis=0) for h in range(heads)]
        for h in range(heads):
            s1 = scores(h, qxs[h], d0, half)
            s1 = jnp.concatenate([s1[:tq] + corr2[h], s1[tq:]], axis=0)
            carry[h] = update(h, s1, carry[h], d0, half)
        outs = []
        for h in range(heads):
            m, l, acc = carry[h]
            s2 = scores(h, qxs[h][tq:], d0 + half, half) + corr2[h]
            _, l_b, acc_b = update(h, s2, (m[tq:], l[tq:], acc[tq:]), d0 + half, half)
            o_a = acc[:tq] / l[:tq]
            o_b = acc_b / l_b
            d = jnp.concatenate([o_a[:half] - lam * o_a[half:], o_b[:half] - lam * o_b[half:]],
                                axis=0)
            ms = jnp.mean(d * d, axis=-1, keepdims=True)
            outs.append(d * lax.rsqrt(ms + RMS_EPS) * sg_ref[...] * (1.0 - lambda_init))
        o_ref[0] = jnp.concatenate(outs, axis=1).astype(o_ref.dtype)

    for n in range(t_all // tq):
        pl.when(i == n)(functools.partial(attend, n))


def _diff_attention(proj, qg2, kg2, sg, lq1, lk1, lq2, lk2, slopes, lambda_init, tq, heads):
    bsz, t, _ = proj.shape
    width = heads * B_V_DIM
    q_blk = A_COLS // width
    k_blk = (A_COLS + B_QK_COLS) // width
    v_blk = (A_COLS + 2 * B_QK_COLS) // width
    vec = lambda n: pl.BlockSpec((1, n), lambda b, h, i: (0, 0))
    return pl.pallas_call(
        functools.partial(_attn_kernel, tq=tq, heads=heads, lambda_init=lambda_init),
        grid=(bsz, B_HEADS // heads, t // tq),
        in_specs=[
            pl.BlockSpec((1, tq, width), lambda b, h, i: (b, i, q_blk + h)),
            pl.BlockSpec((1, t, width), lambda b, h, i: (b, 0, k_blk + h)),
            pl.BlockSpec((1, t, width), lambda b, h, i: (b, 0, v_blk + h)),
            vec(B_V_DIM), vec(B_V_DIM), vec(B_V_DIM),
            vec(B_QK_DIM), vec(B_QK_DIM), vec(B_QK_DIM), vec(B_QK_DIM),
            pl.BlockSpec((heads, 1, V7X_LANES), lambda b, h, i: (h, 0, 0)),
            pl.BlockSpec((heads, t, V7X_LANES), lambda b, h, i: (h, 0, 0)),
        ],
        out_specs=pl.BlockSpec((1, tq, width), lambda b, h, i: (b, i, h)),
        out_shape=jax.ShapeDtypeStruct((bsz, t, B_WIDTH), BF16),
        scratch_shapes=[pltpu.VMEM((heads, t, B_V_DIM + V7X_LANES), BF16),
                        pltpu.VMEM((heads, t, B_V_DIM), BF16),
                        pltpu.VMEM((heads, tq // 2, tq // 2), F32)],
        compiler_params=_params("arbitrary", "arbitrary", "arbitrary"),
        name="diff_attention",
    )(proj, proj, proj, qg2, kg2, sg, lq1, lk1, lq2, lk2, slopes, _alibi_key_bias(slopes, t))


def _conv_kernel(pc_ref, w_ref, b_ref, lw_ref, lb_ref, o_ref, hbuf_ref, *, tile, rows):
    @pl.when(pl.program_id(1) == 0)
    def _():
        hbuf_ref[0, 0:CONV_HALO, :] = jnp.zeros((CONV_HALO, C_WIDTH), F32)

    pc = pc_ref[0]
    n = CONV_HALO + tile
    hbuf_ref[0, CONV_HALO:n, :] = pc[:, :C_WIDTH] * _sigmoid(pc[:, C_WIDTH:])
    full = hbuf_ref[0]
    for s in range(1, V7X_SUBLANES):
        hbuf_ref[s] = pltpu.roll(full, n - s, 0)
    lead = CONV_HALO - (CONV_WIDTH - 1)
    for c in range(tile // rows):
        acc = jnp.zeros((rows, C_WIDTH), F32)
        for j in range(CONV_WIDTH):
            start = c * rows + lead + j
            s = start % V7X_SUBLANES
            acc = acc + w_ref[j:j + 1, :] * hbuf_ref[s, start - s:start - s + rows, :]
        acc = acc + b_ref[...]
        mu = jnp.mean(acc, axis=-1, keepdims=True)
        d = acc - mu
        var = jnp.mean(d * d, axis=-1, keepdims=True)
        y = d * lax.rsqrt(var + LN_EPS) * lw_ref[...] + lb_ref[...]
        o_ref[0, c * rows:(c + 1) * rows, :] = (y * _sigmoid(y)).astype(o_ref.dtype)
    hbuf_ref[0, 0:CONV_HALO, :] = hbuf_ref[0, tile:n, :]


def _conformer_conv(proj, conv_w, conv_b, ln_w, ln_b, tile):
    bsz, t, _ = proj.shape
    c_blk = (A_COLS + B_COLS) // C_COLS
    vec = lambda n: pl.BlockSpec((1, n), lambda b, i: (0, 0))
    return pl.pallas_call(
        functools.partial(_conv_kernel, tile=tile, rows=64),
        grid=(bsz, t // tile),
        in_specs=[
            pl.BlockSpec((1, tile, C_COLS), lambda b, i: (b, i, c_blk)),
            pl.BlockSpec((CONV_WIDTH, C_WIDTH), lambda b, i: (0, 0)),
            vec(C_WIDTH), vec(C_WIDTH), vec(C_WIDTH),
        ],
        out_specs=pl.BlockSpec((1, tile, C_WIDTH), lambda b, i: (b, i, 0)),
        out_shape=jax.ShapeDtypeStruct((bsz, t, C_WIDTH), BF16),
        scratch_shapes=[pltpu.VMEM((V7X_SUBLANES, CONV_HALO + tile, C_WIDTH), F32)],
        compiler_params=_params("arbitrary", "arbitrary"),
        name="conformer_conv",
    )(proj, conv_w, conv_b, ln_w, ln_b)


def _mix_mlp_kernel(x_ref, ya_ref, yb_ref, yc_ref, wo_ref, g1_ref, g_ref, sc_ref, sh_ref, g2_ref,
                    w1_ref, w2_ref, o_ref, *, tf):
    y = jnp.dot(ya_ref[0], wo_ref[0:A_WIDTH, :], preferred_element_type=F32)
    y = y + jnp.dot(yb_ref[0], wo_ref[A_WIDTH:A_WIDTH + B_WIDTH, :], preferred_element_type=F32)
    y = y + jnp.dot(yc_ref[0], wo_ref[A_WIDTH + B_WIDTH:, :], preferred_element_type=F32)
    xn = x_ref[0] + g1_ref[0] * y
    h = _norm_mod(xn, g_ref[...], sc_ref[0], sh_ref[0]).astype(BF16)
    acc = None
    for f in range(w1_ref.shape[1] // tf):
        a = jnp.dot(h, w1_ref[:, f * tf:(f + 1) * tf], preferred_element_type=F32)
        a = jnp.maximum(a, 0.0)
        part = jnp.dot((a * a).astype(BF16), w2_ref[f * tf:(f + 1) * tf, :],
                       preferred_element_type=F32)
        acc = part if acc is None else acc + part
    o_ref[0] = xn + g2_ref[0] * acc


def _mix_mlp(x, ya, yb, yc, wo, g1, g, sc, sh, g2, w1, w2, tm, tf):
    bsz, t, d = x.shape
    dff = w1.shape[1]
    row = pl.BlockSpec((1, 1, d), lambda b, i: (b, 0, 0))
    blk = lambda n: pl.BlockSpec((1, tm, n), lambda b, i: (b, i, 0))
    resident = lambda m, n: pl.BlockSpec((m, n), lambda b, i: (0, 0), pipeline_mode=pl.Buffered(1))
    return pl.pallas_call(
        functools.partial(_mix_mlp_kernel, tf=tf),
        grid=(bsz, t // tm),
        in_specs=[
            blk(d), blk(A_WIDTH), blk(B_WIDTH), blk(C_WIDTH),
            resident(d, d),
            row,
            pl.BlockSpec((1, d), lambda b, i: (0, 0)),
            row, row, row,
            resident(d, dff),
            resident(dff, d),
        ],
        out_specs=blk(d),
        out_shape=jax.ShapeDtypeStruct((bsz, t, d), F32),
        compiler_params=_params("arbitrary", "arbitrary"),
        name="mix_mlp",
    )(x, ya, yb, yc, wo, g1, g, sc, sh, g2, w1, w2)


def _alibi_slopes():
    s = jnp.asarray([2.0 ** (-8.0 * (h + 1) / B_HEADS) for h in range(B_HEADS)], F32)
    return jnp.broadcast_to(s[:, None, None], (B_HEADS, 1, V7X_LANES))


def _alibi_key_bias(slopes, t):
    rem = jnp.arange(t, dtype=F32)[None, :, None] * (slopes[:, :, 0:1] * LOG2E)
    pieces = []
    for _ in range(ALIBI_PIECES):
        bits = lax.bitcast_convert_type(rem, jnp.uint32) & jnp.uint32(0xFFFF0000)
        part = lax.bitcast_convert_type(bits, F32)
        pieces.append(part.astype(BF16))
        rem = rem - part
    table = jnp.concatenate(pieces, axis=-1)
    return jnp.pad(table, ((0, 0), (0, 0), (0, V7X_LANES - ALIBI_PIECES)))


def _pad_rows(w, before, after):
    return jnp.pad(w, ((before, after), (0, 0)))


def _layer(x, mod, p, lambda_init, tiles):
    bsz = x.shape[0]
    sh1, sc1, g1, sh2, sc2, g2 = [m.reshape(bsz, 1, D_MODEL) for m in jnp.split(mod, 6, axis=-1)]
    row = lambda a: a.reshape(1, -1)

    proj = _in_projection(x, row(p["norm1_g"]), sc1, sh1, p["w_in"].astype(BF16), tiles["proj"])
    y_a = _rwkv_mix(
        proj, row(p["tshift_mu"]), row(p["decay_w0"]),
        _pad_rows(p["decay_w2"], 0, ICLR_LORA).astype(BF16), row(p["iclr_a0"]),
        _pad_rows(p["iclr_a2"], DECAY_LORA, 0).astype(BF16), p["gate_g2"].astype(BF16),
        row(p["k_k"]), row(p["k_a"]), row(p["r_k"]), row(p["lnx_w"]), row(p["lnx_b"]),
        tiles["rwkv"], RWKV_ROWS)
    two = lambda a: row(jnp.concatenate([a, a]))
    y_b = _diff_attention(
        proj, two(p["q_norm_g"]), two(p["k_norm_g"]), row(p["subln_g"]),
        row(p["lambda_q1"]), row(p["lambda_k1"]), row(p["lambda_q2"]), row(p["lambda_k2"]),
        _alibi_slopes(), lambda_init, tiles["attn"], ATTN_HEADS)
    y_c = _conformer_conv(proj, p["conv_w"], row(p["conv_b"]), row(p["conv_ln_w"]),
                          row(p["conv_ln_b"]), tiles["conv"])
    return _mix_mlp(x, y_a, y_b, y_c, p["w_out"].astype(BF16), g1, row(p["norm2_g"]), sc2, sh2,
                    g2, p["mlp_w1"].astype(BF16), p["mlp_w2"].astype(BF16), tiles["mlp_tm"],
                    tiles["mlp_tf"])


_LAYER_PARAMS = ("norm1_g", "w_in", "tshift_mu", "decay_w0", "decay_w2", "iclr_a0", "iclr_a2",
                 "gate_g2", "k_k", "k_a", "r_k", "lnx_w", "lnx_b", "q_norm_g", "k_norm_g",
                 "lambda_q1", "lambda_k1", "lambda_q2", "lambda_k2", "subln_g", "conv_w",
                 "conv_b", "conv_ln_w", "conv_ln_b", "w_out", "norm2_g", "mlp_w1", "mlp_w2")


def _block(x, c, ada_w, ada_b, params, tiles):
    mod = _ada_modulation(c, ada_w, ada_b)
    for i in range(ada_w.shape[0]):
        lambda_init = 0.8 - 0.6 * math.exp(-0.3 * i)
        x = _layer(x, mod[i], {k: v[i] for k, v in params.items()}, lambda_init, tiles)
    return x


def kernel(x, c, ada_w, ada_b, norm1_g, w_in, tshift_mu, decay_w0, decay_w2, iclr_a0, iclr_a2,
           gate_g2, k_k, k_a, r_k, lnx_w, lnx_b, q_norm_g, k_norm_g, lambda_q1, lambda_k1,
           lambda_q2, lambda_k2, subln_g, conv_w, conv_b, conv_ln_w, conv_ln_b, w_out, norm2_g,
           mlp_w1, mlp_w2):
    values = (norm1_g, w_in, tshift_mu, decay_w0, decay_w2, iclr_a0, iclr_a2, gate_g2, k_k, k_a,
              r_k.reshape(r_k.shape[0], -1), lnx_w, lnx_b, q_norm_g, k_norm_g, lambda_q1,
              lambda_k1, lambda_q2, lambda_k2, subln_g, conv_w, conv_b, conv_ln_w, conv_ln_b,
              w_out, norm2_g, mlp_w1, mlp_w2)
    t = x.shape[1]
    tiles = dict(proj=min(PROJ_TM, t), rwkv=min(RWKV_TILE, t), conv=min(CONV_TILE, t),
                 attn=min(ATTN_TQ, t),
                 mlp_tm=min(MLP_TM, t), mlp_tf=MLP_TF)
    return _block(x, c, ada_w, ada_b, dict(zip(_LAYER_PARAMS, values)), tiles)
```

```python
import functools
import math

import jax
import jax.numpy as jnp
from jax import lax
from jax.experimental import pallas as pl
from jax.experimental.pallas import tpu as pltpu

F32 = jnp.float32
BF16 = jnp.bfloat16

D_MODEL = 1024
DEPTH = 2
CHUNK = 64
A_HEADS = 4
A_HEAD_DIM = 64
A_WIDTH = A_HEADS * A_HEAD_DIM
DECAY_LORA = 64
ICLR_LORA = 64
GATE_LORA = 128
A_COLS = 3 * A_WIDTH + DECAY_LORA + ICLR_LORA + GATE_LORA
B_HEADS = 4
B_QK_DIM = 64
B_V_DIM = 2 * B_QK_DIM
B_WIDTH = B_HEADS * B_V_DIM
B_QK_COLS = B_HEADS * 2 * B_QK_DIM
B_COLS = 2 * B_QK_COLS + B_WIDTH
C_WIDTH = D_MODEL - A_WIDTH - B_WIDTH
CONV_WIDTH = 31
C_COLS = 2 * C_WIDTH
N_IN = A_COLS + B_COLS + C_COLS
D_FF = 4 * D_MODEL
RMS_EPS = 1e-6
LN_EPS = 1e-5
LNX_EPS = 64e-5
NEG_INF = -1e30
EXP_NEG_HALF = math.exp(-0.5)

V7X_LANES = 128
V7X_SUBLANES = 8
V7X_VMEM_LIMIT_BYTES = 56 * 1024 * 1024

RWKV_CHUNK = 64
RWKV_TILE = 256
RWKV_ROWS = 4
ATTN_TQ = 512
ATTN_HEADS = 2
CONV_HALO = 32
PROJ_TM = 512
MLP_TM = 512
MLP_TF = 2048


def _params(*sem):
    return pltpu.CompilerParams(dimension_semantics=sem,
                                vmem_limit_bytes=V7X_VMEM_LIMIT_BYTES)


def _dot(a, b):
    return jnp.dot(a.astype(BF16), b.astype(BF16), preferred_element_type=F32)


def _dot_nt(a, b):
    return lax.dot_general(a.astype(BF16), b.astype(BF16), (((1,), (1,)), ((), ())),
                           preferred_element_type=F32)


def _dot_tn(a, b):
    return lax.dot_general(a.astype(BF16), b.astype(BF16), (((0,), (0,)), ((), ())),
                           preferred_element_type=F32)


def _split_dot(a, b_exact, parts):
    out = None
    rem = a
    for _ in range(parts):
        piece = rem.astype(BF16)
        term = jnp.dot(piece, b_exact, preferred_element_type=F32)
        out = term if out is None else out + term
        rem = rem - piece.astype(F32)
    return out


def _sigmoid(x):
    return 1.0 / (1.0 + jnp.exp(-x))


def _ada_kernel(c_ref, w_ref, b_ref, o_ref):
    c = c_ref[...]
    cond = c * _sigmoid(c)
    o_ref[0] = _dot(cond, w_ref[0]) + b_ref[0]


def _ada_modulation(c, ada_w, ada_b):
    depth, d, n = ada_w.shape
    bsz = c.shape[0]
    tn = 1536
    return pl.pallas_call(
        _ada_kernel,
        grid=(depth, n // tn),
        in_specs=[
            pl.BlockSpec((bsz, d), lambda l, j: (0, 0)),
            pl.BlockSpec((1, d, tn), lambda l, j: (l, 0, j)),
            pl.BlockSpec((1, 1, tn), lambda l, j: (l, 0, j)),
        ],
        out_specs=pl.BlockSpec((1, bsz, tn), lambda l, j: (l, 0, j)),
        out_shape=jax.ShapeDtypeStruct((depth, bsz, n), F32),
        compiler_params=_params("arbitrary", "arbitrary"),
        name="ada_modulation",
    )(c, ada_w, ada_b.reshape(depth, 1, n))


def _norm_mod(x, g, sc, sh):
    ms = jnp.mean(x * x, axis=-1, keepdims=True)
    y = x * lax.rsqrt(ms + RMS_EPS) * g
    return y * (1.0 + sc) + sh


def _conv_module(pc, w_ref, b_ref, lw_ref, lb_ref, o_ref, hbuf_ref, tile, rows):
    @pl.when(pl.program_id(1) == 0)
    def _():
        hbuf_ref[0, 0:CONV_HALO, :] = jnp.zeros((CONV_HALO, C_WIDTH), F32)

    n = CONV_HALO + tile
    hbuf_ref[0, CONV_HALO:n, :] = pc[:, :C_WIDTH] * _sigmoid(pc[:, C_WIDTH:])
    full = hbuf_ref[0]
    for s in range(1, V7X_SUBLANES):
        hbuf_ref[s] = pltpu.roll(full, n - s, 0)
    lead = CONV_HALO - (CONV_WIDTH - 1)
    for c in range(tile // rows):
        acc = jnp.zeros((rows, C_WIDTH), F32)
        for j in range(CONV_WIDTH):
            start = c * rows + lead + j
            s = start % V7X_SUBLANES
            acc = acc + w_ref[j:j + 1, :] * hbuf_ref[s, start - s:start - s + rows, :]
        acc = acc + b_ref[...]
        mu = jnp.mean(acc, axis=-1, keepdims=True)
        d = acc - mu
        var = jnp.mean(d * d, axis=-1, keepdims=True)
        y = d * lax.rsqrt(var + LN_EPS) * lw_ref[...] + lb_ref[...]
        o_ref[0, c * rows:(c + 1) * rows, :] = (y * _sigmoid(y)).astype(o_ref.dtype)
    hbuf_ref[0, 0:CONV_HALO, :] = hbuf_ref[0, tile:n, :]


def _inproj_kernel(x_ref, g_ref, sc_ref, sh_ref, w_ref, cw_ref, cb_ref, clw_ref, clb_ref,
                   o_ref, yc_ref, hbuf_ref, *, tn, rows):
    tile = x_ref.shape[1]
    h = _norm_mod(x_ref[0], g_ref[...], sc_ref[0], sh_ref[0]).astype(BF16)
    c0 = A_COLS + B_COLS
    pc = jnp.dot(h, w_ref[:, c0:], preferred_element_type=F32)
    o_ref[0, :, c0:] = pc
    _conv_module(pc, cw_ref, cb_ref, clw_ref, clb_ref, yc_ref, hbuf_ref, tile, rows)
    for j in range(c0 // tn):
        o_ref[0, :, j * tn:(j + 1) * tn] = jnp.dot(
            h, w_ref[:, j * tn:(j + 1) * tn], preferred_element_type=F32)


def _in_projection_conv(x, g, sc, sh, w, conv_w, conv_b, ln_w, ln_b, tm):
    bsz, t, d = x.shape
    n = w.shape[1]
    vec = lambda k: pl.BlockSpec((1, k), lambda b, i: (0, 0))
    row = pl.BlockSpec((1, 1, d), lambda b, i: (b, 0, 0))
    return pl.pallas_call(
        functools.partial(_inproj_kernel, tn=C_COLS, rows=64),
        grid=(bsz, t // tm),
        in_specs=[
            pl.BlockSpec((1, tm, d), lambda b, i: (b, i, 0)),
            vec(d), row, row,
            pl.BlockSpec((d, n), lambda b, i: (0, 0), pipeline_mode=pl.Buffered(1)),
            pl.BlockSpec((CONV_WIDTH, C_WIDTH), lambda b, i: (0, 0)),
            vec(C_WIDTH), vec(C_WIDTH), vec(C_WIDTH),
        ],
        out_specs=[pl.BlockSpec((1, tm, n), lambda b, i: (b, i, 0)),
                   pl.BlockSpec((1, tm, C_WIDTH), lambda b, i: (b, i, 0))],
        out_shape=[jax.ShapeDtypeStruct((bsz, t, n), F32),
                   jax.ShapeDtypeStruct((bsz, t, C_WIDTH), BF16)],
        scratch_shapes=[pltpu.VMEM((V7X_SUBLANES, CONV_HALO + tm, C_WIDTH), F32)],
        compiler_params=_params("arbitrary", "arbitrary"),
        name="in_projection_conv",
    )(x, g, sc, sh, w, conv_w, conv_b, ln_w, ln_b)


def _rwkv_kernel(pa_ref, mu_ref, w0_ref, w2_ref, a0_ref, a2_ref, g2_ref, kk_ref, ka_ref,
                 rk_ref, lw_ref, lb_ref, ho_ref, tri_ref, o_ref, carry_ref, s_ref, *, rows, tile,
                 chunk):
    aw = A_WIDTH
    hd = A_HEAD_DIM
    n_chunks = tile // chunk

    @pl.when(pl.program_id(1) == 0)
    def _():
        carry_ref[...] = jnp.zeros_like(carry_ref)
        s_ref[...] = jnp.zeros_like(s_ref)

    ri = lax.broadcasted_iota(jnp.int32, (aw, aw), 0)
    ci = lax.broadcasted_iota(jnp.int32, (aw, aw), 1)
    same_head = (ri // hd) == (ci // hd)
    head_ones = ho_ref[...]
    eye_full = ri == ci
    lane_head = lax.broadcasted_iota(jnp.int32, (hd, aw), 1) // hd
    t_c = lax.broadcasted_iota(jnp.int32, (chunk, aw), 0)
    s_c = lax.broadcasted_iota(jnp.int32, (chunk, aw), 1) % chunk
    strict = s_c < t_c
    incl = s_c <= t_c
    eye_c = (s_c == t_c).astype(F32)
    tri = tri_ref[...]
    row = lax.broadcasted_iota(jnp.int32, (tile, 1), 0)

    def head_sum(x, parts=1):
        return _split_dot(x, head_ones, parts)

    def blockdiag(x):
        return jnp.where(same_head, jnp.concatenate([x] * A_HEADS, axis=0), 0.0)

    seq = []
    for bi in range(rows):
        pa = pa_ref[bi]
        crow = bi * V7X_SUBLANES
        prev = jnp.where(row == 0, carry_ref[crow:crow + 1, :], pltpu.roll(pa, 1, 0))
        carry_ref[crow:crow + 1, :] = pa[tile - 1:tile, :]
        xs = pa + mu_ref[...] * (prev - pa)
        r = xs[:, 0:aw]
        k = xs[:, aw:2 * aw]
        v = xs[:, 2 * aw:3 * aw]
        lora = xs[:, 3 * aw:3 * aw + DECAY_LORA + ICLR_LORA]
        dg = xs[:, 3 * aw + DECAY_LORA + ICLR_LORA:]
        u = w0_ref[...] + _dot(jnp.tanh(lora), w2_ref[...])
        logdecay = -EXP_NEG_HALF * _sigmoid(u)
        iclr = _sigmoid(a0_ref[...] + _dot(lora, a2_ref[...]))
        gate = _dot(_sigmoid(dg), g2_ref[...])
        kk = k * kk_ref[...]
        kk = kk * lax.rsqrt(jnp.maximum(head_sum(kk * kk, 2), 1e-24))
        k = k * (1.0 + (iclr - 1.0) * ka_ref[...])
        b = kk * iclr
        cum = None
        rem = logdecay
        for _ in range(3):
            piece = rem.astype(BF16)
            term = jnp.dot(tri, piece, preferred_element_type=F32)
            cum = term if cum is None else cum + term
            rem = rem - piece.astype(F32)
        seq.append(dict(r=r, k=k, v=v, kk=kk, b=b, ld=logdecay, cum=cum, gate=gate))

    chains = [(bi, c) for bi in range(rows) for c in range(n_chunks)]
    ch = []
    for bi, c in chains:
        q = seq[bi]
        sl = slice(c * chunk, (c + 1) * chunk)
        cum_c = q["cum"][sl]
        total = cum_c[chunk - 1:chunk, :]
        w_inv = jnp.exp(-cum_c)
        w_end = jnp.exp(total - cum_c)
        ch.append(dict(
            rt=q["r"][sl] * jnp.exp(cum_c), at=-q["kk"][sl] * jnp.exp(cum_c - q["ld"][sl]),
            bt=q["b"][sl] * w_inv, kt=q["k"][sl] * w_inv, be=q["b"][sl] * w_end,
            ke=q["k"][sl] * w_end, vc=q["v"][sl], total=total))
    for d in ch:
        lhs = jnp.concatenate([d["at"], d["rt"]], axis=0)
        rhs = jnp.concatenate([blockdiag(d["bt"]), blockdiag(d["kt"])], axis=0)
        prod = _dot_nt(lhs, rhs)
        d["low"] = jnp.where(strict, prod[:chunk, :aw], 0.0)
        d["a_ak"] = jnp.where(strict, prod[:chunk, aw:], 0.0)
        d["a_rb"] = jnp.where(incl, prod[chunk:, :aw], 0.0)
        d["a_rk"] = jnp.where(incl, prod[chunk:, aw:], 0.0)
    for d in ch:
        d["pw"] = _dot(d["low"], blockdiag(d["low"]))
        d["ti"] = eye_c + d["low"]
        d["akv"] = _dot(d["a_ak"], blockdiag(d["vc"]))
        d["y_in"] = _dot(d["a_rk"], blockdiag(d["vc"]))
    for _ in range(int(math.log2(chunk)) - 2):
        for d in ch:
            both = _dot(jnp.concatenate([d["pw"], d["ti"]], axis=0), blockdiag(d["pw"]))
            d["pw"] = both[:chunk]
            d["ti"] = d["ti"] + both[chunk:]
    for d in ch:
        d["ti"] = d["ti"] + _dot(d["ti"], blockdiag(d["pw"]))
    for d in ch:
        both = _dot(d["ti"], jnp.concatenate([blockdiag(d["at"]), blockdiag(d["akv"])], axis=1))
        d["ah"] = both[:, :aw]
        d["vh"] = both[:, aw:]
    for d in ch:
        both = _dot(d["a_rb"], jnp.concatenate([blockdiag(d["ah"]), blockdiag(d["vh"])], axis=1))
        d["rh"] = d["rt"] + both[:, :aw]
        d["y_in"] = d["y_in"] + both[:, aw:]
        trans = jnp.where(same_head, _dot_tn(d["ah"], d["be"]), 0.0)
        d["trans"] = trans + jnp.where(eye_full, jnp.exp(d["total"]), 0.0)
        add_full = _dot_tn(jnp.concatenate([d["vh"], d["vc"]], axis=0),
                           jnp.concatenate([d["be"], d["ke"]], axis=0))
        add = None
        for h in range(A_HEADS):
            blk = jnp.where(lane_head == h, add_full[h * hd:(h + 1) * hd], 0.0)
            add = blk if add is None else add + blk
        d["add"] = add

    states = [s_ref[bi * hd:(bi + 1) * hd, :] for bi in range(rows)]
    ys = [[] for _ in range(rows)]
    for c in range(n_chunks):
        for bi in range(rows):
            d = ch[bi * n_chunks + c]
            ys[bi].append(_dot_nt(d["rh"], blockdiag(states[bi])) + d["y_in"])
            states[bi] = _dot(states[bi], d["trans"]) + d["add"]

    inv_n = 1.0 / hd
    for bi in range(rows):
        q = seq[bi]
        s_ref[bi * hd:(bi + 1) * hd, :] = states[bi]
        y = jnp.concatenate(ys[bi], axis=0)
        mu_h = head_sum(y) * inv_n
        yc = y - mu_h
        var_h = head_sum(yc * yc) * inv_n
        yn = yc * lax.rsqrt(var_h + LNX_EPS) * lw_ref[...] + lb_ref[...]
        bonus = head_sum(q["r"] * q["k"] * rk_ref[...]) * q["v"]
        o_ref[bi] = ((yn + bonus) * q["gate"]).astype(o_ref.dtype)


def _rwkv_mix(proj, mu, w0, w2p, a0, a2p, g2, k_k, k_a, r_k, lnx_w, lnx_b, tile, rows):
    bsz, t, _ = proj.shape
    rows = math.gcd(rows, bsz)
    vec = lambda n: pl.BlockSpec((1, n), lambda b, i: (0, 0))
    mat = lambda m, n: pl.BlockSpec((m, n), lambda b, i: (0, 0))
    lane_head = jnp.arange(A_WIDTH) // A_HEAD_DIM
    head_ones = (lane_head[:, None] == lane_head[None, :]).astype(BF16)
    pos = jnp.arange(tile)
    tri = ((pos[None, :] <= pos[:, None])
           & (pos[None, :] // RWKV_CHUNK == pos[:, None] // RWKV_CHUNK)).astype(BF16)
    return pl.pallas_call(
        functools.partial(_rwkv_kernel, rows=rows, tile=tile, chunk=RWKV_CHUNK),
        grid=(bsz // rows, t // tile),
        in_specs=[
            pl.BlockSpec((rows, tile, A_COLS), lambda b, i: (b, i, 0)),
            vec(A_COLS), vec(A_WIDTH), mat(DECAY_LORA + ICLR_LORA, A_WIDTH),
            vec(A_WIDTH), mat(DECAY_LORA + ICLR_LORA, A_WIDTH), mat(GATE_LORA, A_WIDTH),
            vec(A_WIDTH), vec(A_WIDTH), vec(A_WIDTH), vec(A_WIDTH), vec(A_WIDTH),
            mat(A_WIDTH, A_WIDTH), mat(tile, tile),
        ],
        out_specs=pl.BlockSpec((rows, tile, A_WIDTH), lambda b, i: (b, i, 0)),
        out_shape=jax.ShapeDtypeStruct((bsz, t, A_WIDTH), BF16),
        scratch_shapes=[pltpu.VMEM((rows * V7X_SUBLANES, A_COLS), F32),
                        pltpu.VMEM((rows * A_HEAD_DIM, A_WIDTH), F32)],
        compiler_params=_params("arbitrary", "arbitrary"),
        name="rwkv7_mix",
    )(proj, mu, w0, w2p, a0, a2p, g2, k_k, k_a, r_k, lnx_w, lnx_b, head_ones, tri)


LOG2E = math.log2(math.e)
ALIBI_PIECES = 3


def _half_rms(x, gain, scale):
    ri = lax.broadcasted_iota(jnp.int32, (x.shape[1], x.shape[1]), 0)
    ci = lax.broadcasted_iota(jnp.int32, (x.shape[1], x.shape[1]), 1)
    half_ones = ((ri // B_QK_DIM) == (ci // B_QK_DIM)).astype(F32).astype(BF16)
    ms = _split_dot(x * x, half_ones, 1) * (1.0 / B_QK_DIM)
    return x * lax.rsqrt(ms + RMS_EPS) * (gain * scale)


def _lane_fold(x, op):
    parts = [x[:, k * V7X_LANES:(k + 1) * V7X_LANES] for k in range(x.shape[1] // V7X_LANES)]
    while len(parts) > 1:
        pairs = [op(parts[a], parts[a + 1]) for a in range(0, len(parts) - 1, 2)]
        parts = pairs + parts[len(parts) - len(parts) % 2:]
    return parts[0]


def _attn_kernel(q_ref, k_ref, v_ref, qg_ref, kg_ref, sg_ref, lq1_ref, lk1_ref, lq2_ref,
                 lk2_ref, slope_ref, bias_ref, o_ref, kx_ref, vb_ref, corr_ref, *, tq, heads,
                 lambda_init):
    i = pl.program_id(2)
    t_all = k_ref.shape[1]
    half = tq // 2
    hw = B_V_DIM
    hsl = [slice(h * hw, (h + 1) * hw) for h in range(heads)]

    @pl.when(i == 0)
    def _():
        r = lax.broadcasted_iota(jnp.int32, (half, half), 0)
        c = lax.broadcasted_iota(jnp.int32, (half, half), 1)
        for h in range(heads):
            slope2 = slope_ref[h][:, 0:1] * LOG2E
            kx_ref[h, :, 0:hw] = _half_rms(k_ref[0, :, hsl[h]], kg_ref[...], 1.0).astype(BF16)
            kx_ref[h, :, hw:] = bias_ref[h]
            vb_ref[h] = v_ref[0, :, hsl[h]].astype(BF16)
            after = jnp.where(c > r, (c - r).astype(F32) * (-2.0 * slope2), 0.0)
            corr_ref[h] = jnp.where((c // CHUNK) <= (r // CHUNK), after, NEG_INF)

    lane = lax.broadcasted_iota(jnp.int32, (tq, V7X_LANES), 1)
    lo = lane < B_QK_DIM
    ones = (lane < ALIBI_PIECES).astype(F32)
    qxs = []
    for h in range(heads):
        qn = _half_rms(q_ref[0, :, hsl[h]], qg_ref[...], LOG2E * B_QK_DIM ** -0.5)
        q0 = jnp.concatenate([jnp.where(lo, qn, 0.0), ones], axis=1)
        q1 = jnp.concatenate([jnp.where(lo, 0.0, qn), ones], axis=1)
        qxs.append(jnp.concatenate([q0[:half], q1[:half], q0[half:], q1[half:]],
                                   axis=0).astype(BF16))

    lam = (jnp.exp(jnp.sum(lq1_ref[...] * lk1_ref[...], axis=-1, keepdims=True))
           - jnp.exp(jnp.sum(lq2_ref[...] * lk2_ref[...], axis=-1, keepdims=True))
           + lambda_init)

    def scores(h, lhs, start, size):
        return lax.dot_general(lhs, kx_ref[h, start:start + size, :], (((1,), (1,)), ((), ())),
                               preferred_element_type=F32)

    def update(h, s, carry, start, size):
        m, l, acc = carry
        m_new = jnp.maximum(m, jnp.max(s, axis=-1, keepdims=True))
        alpha = jnp.exp2(m - m_new)
        p = jnp.exp2(s - m_new)
        l = alpha * l + jnp.sum(p, axis=-1, keepdims=True)
        pv = jnp.dot(p.astype(BF16), vb_ref[h, start:start + size, :], preferred_element_type=F32)
        return m_new, l, alpha * acc + pv

    def attend(n):
        d0 = n * tq
        carry = [(jnp.full((2 * tq, 1), NEG_INF, F32), jnp.zeros((2 * tq, 1), F32),
                  jnp.zeros((2 * tq, B_V_DIM), F32)) for _ in range(heads)]
        for j in range(n):
            for h in range(heads):
                carry[h] = update(h, scores(h, qxs[h], j * tq, tq), carry[h], j * tq, tq)
        corr2 = [jnp.concatenate([corr_ref[h], corr_ref[h]], axis=0) for h in range(heads)]
        for h in range(heads):
            s1 = scores(h, qxs[h], d0, half)
            s1 = jnp.concatenate([s1[:tq] + corr2[h], s1[tq:]], axis=0)
            carry[h] = update(h, s1, carry[h], d0, half)
        outs = []
        for h in range(heads):
            m, l, acc = carry[h]
            s2 = scores(h, qxs[h][tq:], d0 + half, half) + corr2[h]
            _, l_b, acc_b = update(h, s2, (m[tq:], l[tq:], acc[tq:]), d0 + half, half)
            o_a = acc[:tq] / l[:tq]
            o_b = acc_b / l_b
            d = jnp.concatenate([o_a[:half] - lam * o_a[half:], o_b[:half] - lam * o_b[half:]],
                                axis=0)
            ms = jnp.mean(d * d, axis=-1, keepdims=True)
            outs.append(d * lax.rsqrt(ms + RMS_EPS) * sg_ref[...] * (1.0 - lambda_init))
        o_ref[0] = jnp.concatenate(outs, axis=1).astype(o_ref.dtype)

    for n in range(t_all // tq):
        pl.when(i == n)(functools.partial(attend, n))


def _diff_attention(proj, qg2, kg2, sg, lq1, lk1, lq2, lk2, slopes, lambda_init, tq, heads):
    bsz, t, _ = proj.shape
    width = heads * B_V_DIM
    q_blk = A_COLS // width
    k_blk = (A_COLS + B_QK_COLS) // width
    v_blk = (A_COLS + 2 * B_QK_COLS) // width
    vec = lambda n: pl.BlockSpec((1, n), lambda b, h, i: (0, 0))
    return pl.pallas_call(
        functools.partial(_attn_kernel, tq=tq, heads=heads, lambda_init=lambda_init),
        grid=(bsz, B_HEADS // heads, t // tq),
        in_specs=[
            pl.BlockSpec((1, tq, width), lambda b, h, i: (b, i, q_blk + h)),
            pl.BlockSpec((1, t, width), lambda b, h, i: (b, 0, k_blk + h)),
            pl.BlockSpec((1, t, width), lambda b, h, i: (b, 0, v_blk + h)),
            vec(B_V_DIM), vec(B_V_DIM), vec(B_V_DIM),
            vec(B_QK_DIM), vec(B_QK_DIM), vec(B_QK_DIM), vec(B_QK_DIM),
            pl.BlockSpec((heads, 1, V7X_LANES), lambda b, h, i: (h, 0, 0)),
            pl.BlockSpec((heads, t, V7X_LANES), lambda b, h, i: (h, 0, 0)),
        ],
        out_specs=pl.BlockSpec((1, tq, width), lambda b, h, i: (b, i, h)),
        out_shape=jax.ShapeDtypeStruct((bsz, t, B_WIDTH), BF16),
        scratch_shapes=[pltpu.VMEM((heads, t, B_V_DIM + V7X_LANES), BF16),
                        pltpu.VMEM((heads, t, B_V_DIM), BF16),
                        pltpu.VMEM((heads, tq // 2, tq // 2), F32)],
        compiler_params=_params("arbitrary", "arbitrary", "arbitrary"),
        name="diff_attention",
    )(proj, proj, proj, qg2, kg2, sg, lq1, lk1, lq2, lk2, slopes, _alibi_key_bias(slopes, t))


def _mix_mlp_kernel(x_ref, ya_ref, yb_ref, yc_ref, wo_ref, g1_ref, g_ref, sc_ref, sh_ref, g2_ref,
                    w1_ref, w2_ref, o_ref, *, tf):
    y = jnp.dot(ya_ref[0], wo_ref[0:A_WIDTH, :], preferred_element_type=F32)
    y = y + jnp.dot(yb_ref[0], wo_ref[A_WIDTH:A_WIDTH + B_WIDTH, :], preferred_element_type=F32)
    y = y + jnp.dot(yc_ref[0], wo_ref[A_WIDTH + B_WIDTH:, :], preferred_element_type=F32)
    xn = x_ref[0] + g1_ref[0] * y
    h = _norm_mod(xn, g_ref[...], sc_ref[0], sh_ref[0]).astype(BF16)
    acc = None
    for f in range(w1_ref.shape[1] // tf):
        a = jnp.dot(h, w1_ref[:, f * tf:(f + 1) * tf], preferred_element_type=F32)
        a = jnp.maximum(a, 0.0)
        part = jnp.dot((a * a).astype(BF16), w2_ref[f * tf:(f + 1) * tf, :],
                       preferred_element_type=F32)
        acc = part if acc is None else acc + part
    o_ref[0] = xn + g2_ref[0] * acc


def _mix_mlp(x, ya, yb, yc, wo, g1, g, sc, sh, g2, w1, w2, tm, tf):
    bsz, t, d = x.shape
    dff = w1.shape[1]
    row = pl.BlockSpec((1, 1, d), lambda b, i: (b, 0, 0))
    blk = lambda n: pl.BlockSpec((1, tm, n), lambda b, i: (b, i, 0))
    resident = lambda m, n: pl.BlockSpec((m, n), lambda b, i: (0, 0), pipeline_mode=pl.Buffered(1))
    return pl.pallas_call(
        functools.partial(_mix_mlp_kernel, tf=tf),
        grid=(bsz, t // tm),
        in_specs=[
            blk(d), blk(A_WIDTH), blk(B_WIDTH), blk(C_WIDTH),
            resident(d, d),
            row,
            pl.BlockSpec((1, d), lambda b, i: (0, 0)),
            row, row, row,
            resident(d, dff),
            resident(dff, d),
        ],
        out_specs=blk(d),
        out_shape=jax.ShapeDtypeStruct((bsz, t, d), F32),
        compiler_params=_params("arbitrary", "arbitrary"),
        name="mix_mlp",
    )(x, ya, yb, yc, wo, g1, g, sc, sh, g2, w1, w2)


def _alibi_slopes():
    s = jnp.asarray([2.0 ** (-8.0 * (h + 1) / B_HEADS) for h in range(B_HEADS)], F32)
    return jnp.broadcast_to(s[:, None, None], (B_HEADS, 1, V7X_LANES))


def _alibi_key_bias(slopes, t):
    rem = jnp.arange(t, dtype=F32)[None, :, None] * (slopes[:, :, 0:1] * LOG2E)
    pieces = []
    for _ in range(ALIBI_PIECES):
        bits = lax.bitcast_convert_type(rem, jnp.uint32) & jnp.uint32(0xFFFF0000)
        part = lax.bitcast_convert_type(bits, F32)
        pieces.append(part.astype(BF16))
        rem = rem - part
    table = jnp.concatenate(pieces, axis=-1)
    return jnp.pad(table, ((0, 0), (0, 0), (0, V7X_LANES - ALIBI_PIECES)))


def _pad_rows(w, before, after):
    return jnp.pad(w, ((before, after), (0, 0)))


def _layer(x, mod, p, lambda_init, tiles):
    bsz = x.shape[0]
    sh1, sc1, g1, sh2, sc2, g2 = [m.reshape(bsz, 1, D_MODEL) for m in jnp.split(mod, 6, axis=-1)]
    row = lambda a: a.reshape(1, -1)

    proj, y_c = _in_projection_conv(
        x, row(p["norm1_g"]), sc1, sh1, p["w_in"].astype(BF16), p["conv_w"], row(p["conv_b"]),
        row(p["conv_ln_w"]), row(p["conv_ln_b"]), tiles["proj"])
    y_a = _rwkv_mix(
        proj, row(p["tshift_mu"]), row(p["decay_w0"]),
        _pad_rows(p["decay_w2"], 0, ICLR_LORA).astype(BF16), row(p["iclr_a0"]),
        _pad_rows(p["iclr_a2"], DECAY_LORA, 0).astype(BF16), p["gate_g2"].astype(BF16),
        row(p["k_k"]), row(p["k_a"]), row(p["r_k"]), row(p["lnx_w"]), row(p["lnx_b"]),
        tiles["rwkv"], RWKV_ROWS)
    two = lambda a: row(jnp.concatenate([a, a]))
    y_b = _diff_attention(
        proj, two(p["q_norm_g"]), two(p["k_norm_g"]), row(p["subln_g"]),
        row(p["lambda_q1"]), row(p["lambda_k1"]), row(p["lambda_q2"]), row(p["lambda_k2"]),
        _alibi_slopes(), lambda_init, tiles["attn"], ATTN_HEADS)
    return _mix_mlp(x, y_a, y_b, y_c, p["w_out"].astype(BF16), g1, row(p["norm2_g"]), sc2, sh2,
                    g2, p["mlp_w1"].astype(BF16), p["mlp_w2"].astype(BF16), tiles["mlp_tm"],
                    tiles["mlp_tf"])


_LAYER_PARAMS = ("norm1_g", "w_in", "tshift_mu", "decay_w0", "decay_w2", "iclr_a0", "iclr_a2",
                 "gate_g2", "k_k", "k_a", "r_k", "lnx_w", "lnx_b", "q_norm_g", "k_norm_g",
                 "lambda_q1", "lambda_k1", "lambda_q2", "lambda_k2", "subln_g", "conv_w",
                 "conv_b", "conv_ln_w", "conv_ln_b", "w_out", "norm2_g", "mlp_w1", "mlp_w2")


def _block(x, c, ada_w, ada_b, params, tiles):
    mod = _ada_modulation(c, ada_w, ada_b)
    for i in range(ada_w.shape[0]):
        lambda_init = 0.8 - 0.6 * math.exp(-0.3 * i)
        x = _layer(x, mod[i], {k: v[i] for k, v in params.items()}, lambda_init, tiles)
    return x


def kernel(x, c, ada_w, ada_b, norm1_g, w_in, tshift_mu, decay_w0, decay_w2, iclr_a0, iclr_a2,
           gate_g2, k_k, k_a, r_k, lnx_w, lnx_b, q_norm_g, k_norm_g, lambda_q1, lambda_k1,
           lambda_q2, lambda_k2, subln_g, conv_w, conv_b, conv_ln_w, conv_ln_b, w_out, norm2_g,
           mlp_w1, mlp_w2):
    values = (norm1_g, w_in, tshift_mu, decay_w0, decay_w2, iclr_a0, iclr_a2, gate_g2, k_k, k_a,
              r_k.reshape(r_k.shape[0], -1), lnx_w, lnx_b, q_norm_g, k_norm_g, lambda_q1,
              lambda_k1, lambda_q2, lambda_k2, subln_g, conv_w, conv_b, conv_ln_w, conv_ln_b,
              w_out, norm2_g, mlp_w1, mlp_w2)
    t = x.shape[1]
    tiles = dict(proj=min(PROJ_TM, t), rwkv=min(RWKV_TILE, t), attn=min(ATTN_TQ, t),
                 mlp_tm=min(MLP_TM, t), mlp_tf=MLP_TF)
    return _block(x, c, ada_w, ada_b, dict(zip(_LAYER_PARAMS, values)), tiles)
```

```python
import functools
import math

import jax
import jax.numpy as jnp
from jax import lax
from jax.experimental import pallas as pl
from jax.experimental.pallas import tpu as pltpu

F32 = jnp.float32
BF16 = jnp.bfloat16

D_MODEL = 1024
DEPTH = 2
CHUNK = 64
A_HEADS = 4
A_HEAD_DIM = 64
A_WIDTH = A_HEADS * A_HEAD_DIM
DECAY_LORA = 64
ICLR_LORA = 64
GATE_LORA = 128
A_COLS = 3 * A_WIDTH + DECAY_LORA + ICLR_LORA + GATE_LORA
B_HEADS = 4
B_QK_DIM = 64
B_V_DIM = 2 * B_QK_DIM
B_WIDTH = B_HEADS * B_V_DIM
B_QK_COLS = B_HEADS * 2 * B_QK_DIM
B_COLS = 2 * B_QK_COLS + B_WIDTH
C_WIDTH = D_MODEL - A_WIDTH - B_WIDTH
CONV_WIDTH = 31
C_COLS = 2 * C_WIDTH
N_IN = A_COLS + B_COLS + C_COLS
D_FF = 4 * D_MODEL
RMS_EPS = 1e-6
LN_EPS = 1e-5
LNX_EPS = 64e-5
NEG_INF = -1e30
EXP_NEG_HALF = math.exp(-0.5)

V7X_LANES = 128
V7X_SUBLANES = 8
V7X_VMEM_LIMIT_BYTES = 56 * 1024 * 1024

RWKV_CHUNK = 64
RWKV_TILE = 256
RWKV_ROWS = 4
ATTN_TQ = 512
ATTN_HEADS = 4
CONV_HALO = 32
PROJ_TM = 512
MLP_TM = 512
MLP_TF = 2048


def _params(*sem):
    return pltpu.CompilerParams(dimension_semantics=sem,
                                vmem_limit_bytes=V7X_VMEM_LIMIT_BYTES)


def _dot(a, b):
    return jnp.dot(a.astype(BF16), b.astype(BF16), preferred_element_type=F32)


def _dot_nt(a, b):
    return lax.dot_general(a.astype(BF16), b.astype(BF16), (((1,), (1,)), ((), ())),
                           preferred_element_type=F32)


def _dot_tn(a, b):
    return lax.dot_general(a.astype(BF16), b.astype(BF16), (((0,), (0,)), ((), ())),
                           preferred_element_type=F32)


def _split_dot(a, b_exact, parts):
    out = None
    rem = a
    for _ in range(parts):
        piece = rem.astype(BF16)
        term = jnp.dot(piece, b_exact, preferred_element_type=F32)
        out = term if out is None else out + term
        rem = rem - piece.astype(F32)
    return out


def _sigmoid(x):
    return 1.0 / (1.0 + jnp.exp(-x))


def _ada_kernel(c_ref, w_ref, b_ref, o_ref):
    c = c_ref[...]
    cond = c * _sigmoid(c)
    o_ref[0] = _dot(cond, w_ref[0]) + b_ref[0]


def _ada_modulation(c, ada_w, ada_b):
    depth, d, n = ada_w.shape
    bsz = c.shape[0]
    tn = 1536
    return pl.pallas_call(
        _ada_kernel,
        grid=(depth, n // tn),
        in_specs=[
            pl.BlockSpec((bsz, d), lambda l, j: (0, 0)),
            pl.BlockSpec((1, d, tn), lambda l, j: (l, 0, j)),
            pl.BlockSpec((1, 1, tn), lambda l, j: (l, 0, j)),
        ],
        out_specs=pl.BlockSpec((1, bsz, tn), lambda l, j: (l, 0, j)),
        out_shape=jax.ShapeDtypeStruct((depth, bsz, n), F32),
        compiler_params=_params("arbitrary", "arbitrary"),
        name="ada_modulation",
    )(c, ada_w, ada_b.reshape(depth, 1, n))


def _norm_mod(x, g, sc, sh):
    ms = jnp.mean(x * x, axis=-1, keepdims=True)
    y = x * lax.rsqrt(ms + RMS_EPS) * g
    return y * (1.0 + sc) + sh


def _conv_module(pc, w_ref, b_ref, lw_ref, lb_ref, o_ref, hbuf_ref, tile, rows):
    @pl.when(pl.program_id(1) == 0)
    def _():
        hbuf_ref[0, 0:CONV_HALO, :] = jnp.zeros((CONV_HALO, C_WIDTH), F32)

    n = CONV_HALO + tile
    hbuf_ref[0, CONV_HALO:n, :] = pc[:, :C_WIDTH] * _sigmoid(pc[:, C_WIDTH:])
    full = hbuf_ref[0]
    for s in range(1, V7X_SUBLANES):
        hbuf_ref[s] = pltpu.roll(full, n - s, 0)
    lead = CONV_HALO - (CONV_WIDTH - 1)
    for c in range(tile // rows):
        acc = jnp.zeros((rows, C_WIDTH), F32)
        for j in range(CONV_WIDTH):
            start = c * rows + lead + j
            s = start % V7X_SUBLANES
            acc = acc + w_ref[j:j + 1, :] * hbuf_ref[s, start - s:start - s + rows, :]
        acc = acc + b_ref[...]
        mu = jnp.mean(acc, axis=-1, keepdims=True)
        d = acc - mu
        var = jnp.mean(d * d, axis=-1, keepdims=True)
        y = d * lax.rsqrt(var + LN_EPS) * lw_ref[...] + lb_ref[...]
        o_ref[0, c * rows:(c + 1) * rows, :] = (y * _sigmoid(y)).astype(o_ref.dtype)
    hbuf_ref[0, 0:CONV_HALO, :] = hbuf_ref[0, tile:n, :]


def _inproj_kernel(x_ref, g_ref, sc_ref, sh_ref, w_ref, cw_ref, cb_ref, clw_ref, clb_ref,
                   o_ref, yc_ref, hbuf_ref, *, tn, rows):
    tile = x_ref.shape[1]
    h = _norm_mod(x_ref[0], g_ref[...], sc_ref[0], sh_ref[0]).astype(BF16)
    c0 = A_COLS + B_COLS
    pc = jnp.dot(h, w_ref[:, c0:], preferred_element_type=F32)
    o_ref[0, :, c0:] = pc
    _conv_module(pc, cw_ref, cb_ref, clw_ref, clb_ref, yc_ref, hbuf_ref, tile, rows)
    for j in range(c0 // tn):
        o_ref[0, :, j * tn:(j + 1) * tn] = jnp.dot(
            h, w_ref[:, j * tn:(j + 1) * tn], preferred_element_type=F32)


def _in_projection_conv(x, g, sc, sh, w, conv_w, conv_b, ln_w, ln_b, tm):
    bsz, t, d = x.shape
    n = w.shape[1]
    vec = lambda k: pl.BlockSpec((1, k), lambda b, i: (0, 0))
    row = pl.BlockSpec((1, 1, d), lambda b, i: (b, 0, 0))
    return pl.pallas_call(
        functools.partial(_inproj_kernel, tn=C_COLS, rows=64),
        grid=(bsz, t // tm),
        in_specs=[
            pl.BlockSpec((1, tm, d), lambda b, i: (b, i, 0)),
            vec(d), row, row,
            pl.BlockSpec((d, n), lambda b, i: (0, 0), pipeline_mode=pl.Buffered(1)),
            pl.BlockSpec((CONV_WIDTH, C_WIDTH), lambda b, i: (0, 0)),
            vec(C_WIDTH), vec(C_WIDTH), vec(C_WIDTH),
        ],
        out_specs=[pl.BlockSpec((1, tm, n), lambda b, i: (b, i, 0)),
                   pl.BlockSpec((1, tm, C_WIDTH), lambda b, i: (b, i, 0))],
        out_shape=[jax.ShapeDtypeStruct((bsz, t, n), F32),
                   jax.ShapeDtypeStruct((bsz, t, C_WIDTH), BF16)],
        scratch_shapes=[pltpu.VMEM((V7X_SUBLANES, CONV_HALO + tm, C_WIDTH), F32)],
        compiler_params=_params("arbitrary", "arbitrary"),
        name="in_projection_conv",
    )(x, g, sc, sh, w, conv_w, conv_b, ln_w, ln_b)


def _rwkv_kernel(pa_ref, mu_ref, w0_ref, w2_ref, a0_ref, a2_ref, g2_ref, kk_ref, ka_ref,
                 rk_ref, lw_ref, lb_ref, ho_ref, tri_ref, o_ref, carry_ref, s_ref, *, rows, tile,
                 chunk):
    aw = A_WIDTH
    hd = A_HEAD_DIM
    n_chunks = tile // chunk

    @pl.when(pl.program_id(1) == 0)
    def _():
        carry_ref[...] = jnp.zeros_like(carry_ref)
        s_ref[...] = jnp.zeros_like(s_ref)

    ri = lax.broadcasted_iota(jnp.int32, (aw, aw), 0)
    ci = lax.broadcasted_iota(jnp.int32, (aw, aw), 1)
    same_head = (ri // hd) == (ci // hd)
    head_ones = ho_ref[...]
    eye_full = ri == ci
    lane_head = lax.broadcasted_iota(jnp.int32, (hd, aw), 1) // hd
    t_c = lax.broadcasted_iota(jnp.int32, (chunk, aw), 0)
    s_c = lax.broadcasted_iota(jnp.int32, (chunk, aw), 1) % chunk
    strict = s_c < t_c
    incl = s_c <= t_c
    eye_c = (s_c == t_c).astype(F32)
    tri = tri_ref[...]
    row = lax.broadcasted_iota(jnp.int32, (tile, 1), 0)

    def head_sum(x, parts=1):
        return _split_dot(x, head_ones, parts)

    def blockdiag(x):
        return jnp.where(same_head, jnp.concatenate([x] * A_HEADS, axis=0), 0.0)

    seq = []
    for bi in range(rows):
        pa = pa_ref[bi]
        crow = bi * V7X_SUBLANES
        prev = jnp.where(row == 0, carry_ref[crow:crow + 1, :], pltpu.roll(pa, 1, 0))
        carry_ref[crow:crow + 1, :] = pa[tile - 1:tile, :]
        xs = pa + mu_ref[...] * (prev - pa)
        r = xs[:, 0:aw]
        k = xs[:, aw:2 * aw]
        v = xs[:, 2 * aw:3 * aw]
        lora = xs[:, 3 * aw:3 * aw + DECAY_LORA + ICLR_LORA]
        dg = xs[:, 3 * aw + DECAY_LORA + ICLR_LORA:]
        u = w0_ref[...] + _dot(jnp.tanh(lora), w2_ref[...])
        logdecay = -EXP_NEG_HALF * _sigmoid(u)
        iclr = _sigmoid(a0_ref[...] + _dot(lora, a2_ref[...]))
        gate = _dot(_sigmoid(dg), g2_ref[...])
        kk = k * kk_ref[...]
        kk = kk * lax.rsqrt(jnp.maximum(head_sum(kk * kk, 2), 1e-24))
        k = k * (1.0 + (iclr - 1.0) * ka_ref[...])
        b = kk * iclr
        cum = None
        rem = logdecay
        for _ in range(3):
            piece = rem.astype(BF16)
            term = jnp.dot(tri, piece, preferred_element_type=F32)
            cum = term if cum is None else cum + term
            rem = rem - piece.astype(F32)
        seq.append(dict(r=r, k=k, v=v, kk=kk, b=b, ld=logdecay, cum=cum, gate=gate))

    chains = [(bi, c) for bi in range(rows) for c in range(n_chunks)]
    ch = []
    for bi, c in chains:
        q = seq[bi]
        sl = slice(c * chunk, (c + 1) * chunk)
        cum_c = q["cum"][sl]
        total = cum_c[chunk - 1:chunk, :]
        w_inv = jnp.exp(-cum_c)
        w_end = jnp.exp(total - cum_c)
        ch.append(dict(
            rt=q["r"][sl] * jnp.exp(cum_c), at=-q["kk"][sl] * jnp.exp(cum_c - q["ld"][sl]),
            bt=q["b"][sl] * w_inv, kt=q["k"][sl] * w_inv, be=q["b"][sl] * w_end,
            ke=q["k"][sl] * w_end, vc=q["v"][sl], total=total))
    for d in ch:
        lhs = jnp.concatenate([d["at"], d["rt"]], axis=0)
        rhs = jnp.concatenate([blockdiag(d["bt"]), blockdiag(d["kt"])], axis=0)
        prod = _dot_nt(lhs, rhs)
        d["low"] = jnp.where(strict, prod[:chunk, :aw], 0.0)
        d["a_ak"] = jnp.where(strict, prod[:chunk, aw:], 0.0)
        d["a_rb"] = jnp.where(incl, prod[chunk:, :aw], 0.0)
        d["a_rk"] = jnp.where(incl, prod[chunk:, aw:], 0.0)
    for d in ch:
        d["pw"] = _dot(d["low"], blockdiag(d["low"]))
        d["ti"] = eye_c + d["low"]
        d["akv"] = _dot(d["a_ak"], blockdiag(d["vc"]))
        d["y_in"] = _dot(d["a_rk"], blockdiag(d["vc"]))
    for _ in range(int(math.log2(chunk)) - 2):
        for d in ch:
            both = _dot(jnp.concatenate([d["pw"], d["ti"]], axis=0), blockdiag(d["pw"]))
            d["pw"] = both[:chunk]
            d["ti"] = d["ti"] + both[chunk:]
    for d in ch:
        d["ti"] = d["ti"] + _dot(d["ti"], blockdiag(d["pw"]))
    for d in ch:
        both = _dot(d["ti"], jnp.concatenate([blockdiag(d["at"]), blockdiag(d["akv"])], axis=1))
        d["ah"] = both[:, :aw]
        d["vh"] = both[:, aw:]
    for d in ch:
        both = _dot(d["a_rb"], jnp.concatenate([blockdiag(d["ah"]), blockdiag(d["vh"])], axis=1))
        d["rh"] = d["rt"] + both[:, :aw]
        d["y_in"] = d["y_in"] + both[:, aw:]
        trans = jnp.where(same_head, _dot_tn(d["ah"], d["be"]), 0.0)
        d["trans"] = trans + jnp.where(eye_full, jnp.exp(d["total"]), 0.0)
        add_full = _dot_tn(jnp.concatenate([d["vh"], d["vc"]], axis=0),
                           jnp.concatenate([d["be"], d["ke"]], axis=0))
        add = None
        for h in range(A_HEADS):
            blk = jnp.where(lane_head == h, add_full[h * hd:(h + 1) * hd], 0.0)
            add = blk if add is None else add + blk
        d["add"] = add

    states = [s_ref[bi * hd:(bi + 1) * hd, :] for bi in range(rows)]
    ys = [[] for _ in range(rows)]
    for c in range(n_chunks):
        for bi in range(rows):
            d = ch[bi * n_chunks + c]
            ys[bi].append(_dot_nt(d["rh"], blockdiag(states[bi])) + d["y_in"])
            states[bi] = _dot(states[bi], d["trans"]) + d["add"]

    inv_n = 1.0 / hd
    for bi in range(rows):
        q = seq[bi]
        s_ref[bi * hd:(bi + 1) * hd, :] = states[bi]
        y = jnp.concatenate(ys[bi], axis=0)
        mu_h = head_sum(y) * inv_n
        yc = y - mu_h
        var_h = head_sum(yc * yc) * inv_n
        yn = yc * lax.rsqrt(var_h + LNX_EPS) * lw_ref[...] + lb_ref[...]
        bonus = head_sum(q["r"] * q["k"] * rk_ref[...]) * q["v"]
        o_ref[bi] = ((yn + bonus) * q["gate"]).astype(o_ref.dtype)


def _rwkv_mix(proj, mu, w0, w2p, a0, a2p, g2, k_k, k_a, r_k, lnx_w, lnx_b, tile, rows):
    bsz, t, _ = proj.shape
    rows = math.gcd(rows, bsz)
    vec = lambda n: pl.BlockSpec((1, n), lambda b, i: (0, 0))
    mat = lambda m, n: pl.BlockSpec((m, n), lambda b, i: (0, 0))
    lane_head = jnp.arange(A_WIDTH) // A_HEAD_DIM
    head_ones = (lane_head[:, None] == lane_head[None, :]).astype(BF16)
    pos = jnp.arange(tile)
    tri = ((pos[None, :] <= pos[:, None])
           & (pos[None, :] // RWKV_CHUNK == pos[:, None] // RWKV_CHUNK)).astype(BF16)
    return pl.pallas_call(
        functools.partial(_rwkv_kernel, rows=rows, tile=tile, chunk=RWKV_CHUNK),
        grid=(bsz // rows, t // tile),
        in_specs=[
            pl.BlockSpec((rows, tile, A_COLS), lambda b, i: (b, i, 0)),
            vec(A_COLS), vec(A_WIDTH), mat(DECAY_LORA + ICLR_LORA, A_WIDTH),
            vec(A_WIDTH), mat(DECAY_LORA + ICLR_LORA, A_WIDTH), mat(GATE_LORA, A_WIDTH),
            vec(A_WIDTH), vec(A_WIDTH), vec(A_WIDTH), vec(A_WIDTH), vec(A_WIDTH),
            mat(A_WIDTH, A_WIDTH), mat(tile, tile),
        ],
        out_specs=pl.BlockSpec((rows, tile, A_WIDTH), lambda b, i: (b, i, 0)),
        out_shape=jax.ShapeDtypeStruct((bsz, t, A_WIDTH), BF16),
        scratch_shapes=[pltpu.VMEM((rows * V7X_SUBLANES, A_COLS), F32),
                        pltpu.VMEM((rows * A_HEAD_DIM, A_WIDTH), F32)],
        compiler_params=_params("arbitrary", "arbitrary"),
        name="rwkv7_mix",
    )(proj, mu, w0, w2p, a0, a2p, g2, k_k, k_a, r_k, lnx_w, lnx_b, head_ones, tri)


LOG2E = math.log2(math.e)
ALIBI_PIECES = 3


def _half_rms(x, gain, scale):
    ri = lax.broadcasted_iota(jnp.int32, (x.shape[1], x.shape[1]), 0)
    ci = lax.broadcasted_iota(jnp.int32, (x.shape[1], x.shape[1]), 1)
    half_ones = ((ri // B_QK_DIM) == (ci // B_QK_DIM)).astype(F32).astype(BF16)
    ms = _split_dot(x * x, half_ones, 1) * (1.0 / B_QK_DIM)
    return x * lax.rsqrt(ms + RMS_EPS) * (gain * scale)


def _lane_fold(x, op):
    parts = [x[:, k * V7X_LANES:(k + 1) * V7X_LANES] for k in range(x.shape[1] // V7X_LANES)]
    while len(parts) > 1:
        pairs = [op(parts[a], parts[a + 1]) for a in range(0, len(parts) - 1, 2)]
        parts = pairs + parts[len(parts) - len(parts) % 2:]
    return parts[0]


def _attn_kernel(q_ref, k_ref, v_ref, qg_ref, kg_ref, sg_ref, lq1_ref, lk1_ref, lq2_ref,
                 lk2_ref, slope_ref, bias_ref, o_ref, kx_ref, vb_ref, corr_ref, *, tq, heads,
                 lambda_init):
    i = pl.program_id(2)
    t_all = k_ref.shape[1]
    half = tq // 2
    hw = B_V_DIM
    hsl = [slice(h * hw, (h + 1) * hw) for h in range(heads)]

    @pl.when(i == 0)
    def _():
        r = lax.broadcasted_iota(jnp.int32, (half, half), 0)
        c = lax.broadcasted_iota(jnp.int32, (half, half), 1)
        for h in range(heads):
            slope2 = slope_ref[h][:, 0:1] * LOG2E
            kx_ref[h, :, 0:hw] = _half_rms(k_ref[0, :, hsl[h]], kg_ref[...], 1.0).astype(BF16)
            kx_ref[h, :, hw:] = bias_ref[h]
            vb_ref[h] = v_ref[0, :, hsl[h]].astype(BF16)
            after = jnp.where(c > r, (c - r).astype(F32) * (-2.0 * slope2), 0.0)
            corr_ref[h] = jnp.where((c // CHUNK) <= (r // CHUNK), after, NEG_INF)

    lane = lax.broadcasted_iota(jnp.int32, (tq, V7X_LANES), 1)
    lo = lane < B_QK_DIM
    ones = (lane < ALIBI_PIECES).astype(F32)
    qxs = []
    for h in range(heads):
        qn = _half_rms(q_ref[0, :, hsl[h]], qg_ref[...], LOG2E * B_QK_DIM ** -0.5)
        q0 = jnp.concatenate([jnp.where(lo, qn, 0.0), ones], axis=1)
        q1 = jnp.concatenate([jnp.where(lo, 0.0, qn), ones], axis=1)
        qxs.append(jnp.concatenate([q0[:half], q1[:half], q0[half:], q1[half:]],
                                   axis=0).astype(BF16))

    lam = (jnp.exp(jnp.sum(lq1_ref[...] * lk1_ref[...], axis=-1, keepdims=True))
           - jnp.exp(jnp.sum(lq2_ref[...] * lk2_ref[...], axis=-1, keepdims=True))
           + lambda_init)

    def scores(h, lhs, start, size):
        return lax.dot_general(lhs, kx_ref[h, start:start + size, :], (((1,), (1,)), ((), ())),
                               preferred_element_type=F32)

    def update(h, s, carry, start, size):
        m, l, acc = carry
        m_new = jnp.maximum(m, jnp.max(s, axis=-1, keepdims=True))
        alpha = jnp.exp2(m - m_new)
        p = jnp.exp2(s - m_new)
        l = alpha * l + jnp.sum(p, axis=-1, keepdims=True)
        pv = jnp.dot(p.astype(BF16), vb_ref[h, start:start + size, :], preferred_element_type=F32)
        return m_new, l, alpha * acc + pv

    def attend(n):
        d0 = n * tq
        carry = [(jnp.full((2 * tq, 1), NEG_INF, F32), jnp.zeros((2 * tq, 1), F32),
                  jnp.zeros((2 * tq, B_V_DIM), F32)) for _ in range(heads)]
        for j in range(n):
            for h in range(heads):
                carry[h] = update(h, scores(h, qxs[h], j * tq, tq), carry[h], j * tq, tq)
        corr2 = [jnp.concatenate([corr_ref[h], corr_ref[h]], axis=0) for h in range(heads)]
        for h in range(heads):
            s1 = scores(h, qxs[h], d0, half)
            s1 = jnp.concatenate([s1[:tq] + corr2[h], s1[tq:]], axis=0)
            carry[h] = update(h, s1, carry[h], d0, half)
        outs = []
        for h in range(heads):
            m, l, acc = carry[h]
            s2 = scores(h, qxs[h][tq:], d0 + half, half) + corr2[h]
            _, l_b, acc_b = update(h, s2, (m[tq:], l[tq:], acc[tq:]), d0 + half, half)
            o_a = acc[:tq] / l[:tq]
            o_b = acc_b / l_b
            d = jnp.concatenate([o_a[:half] - lam * o_a[half:], o_b[:half] - lam * o_b[half:]],
                                axis=0)
            ms = jnp.mean(d * d, axis=-1, keepdims=True)
            outs.append(d * lax.rsqrt(ms + RMS_EPS) * sg_ref[...] * (1.0 - lambda_init))
        o_ref[0] = jnp.concatenate(outs, axis=1).astype(o_ref.dtype)

    for n in range(t_all // tq):
        pl.when(i == n)(functools.partial(attend, n))


def _diff_attention(proj, qg2, kg2, sg, lq1, lk1, lq2, lk2, slopes, lambda_init, tq, heads):
    bsz, t, _ = proj.shape
    width = heads * B_V_DIM
    q_blk = A_COLS // width
    k_blk = (A_COLS + B_QK_COLS) // width
    v_blk = (A_COLS + 2 * B_QK_COLS) // width
    vec = lambda n: pl.BlockSpec((1, n), lambda b, h, i: (0, 0))
    return pl.pallas_call(
        functools.partial(_attn_kernel, tq=tq, heads=heads, lambda_init=lambda_init),
        grid=(bsz, B_HEADS // heads, t // tq),
        in_specs=[
            pl.BlockSpec((1, tq, width), lambda b, h, i: (b, i, q_blk + h)),
            pl.BlockSpec((1, t, width), lambda b, h, i: (b, 0, k_blk + h)),
            pl.BlockSpec((1, t, width), lambda b, h, i: (b, 0, v_blk + h)),
            vec(B_V_DIM), vec(B_V_DIM), vec(B_V_DIM),
            vec(B_QK_DIM), vec(B_QK_DIM), vec(B_QK_DIM), vec(B_QK_DIM),
            pl.BlockSpec((heads, 1, V7X_LANES), lambda b, h, i: (h, 0, 0)),
            pl.BlockSpec((heads, t, V7X_LANES), lambda b, h, i: (h, 0, 0)),
        ],
        out_specs=pl.BlockSpec((1, tq, width), lambda b, h, i: (b, i, h)),
        out_shape=jax.ShapeDtypeStruct((bsz, t, B_WIDTH), BF16),
        scratch_shapes=[pltpu.VMEM((heads, t, B_V_DIM + V7X_LANES), BF16),
                        pltpu.VMEM((heads, t, B_V_DIM), BF16),
                        pltpu.VMEM((heads, tq // 2, tq // 2), F32)],
        compiler_params=_params("arbitrary", "arbitrary", "arbitrary"),
        name="diff_attention",
    )(proj, proj, proj, qg2, kg2, sg, lq1, lk1, lq2, lk2, slopes, _alibi_key_bias(slopes, t))


def _mix_mlp_kernel(x_ref, ya_ref, yb_ref, yc_ref, wo_ref, g1_ref, g_ref, sc_ref, sh_ref, g2_ref,
                    w1_ref, w2_ref, o_ref, *, tf):
    y = jnp.dot(ya_ref[0], wo_ref[0:A_WIDTH, :], preferred_element_type=F32)
    y = y + jnp.dot(yb_ref[0], wo_ref[A_WIDTH:A_WIDTH + B_WIDTH, :], preferred_element_type=F32)
    y = y + jnp.dot(yc_ref[0], wo_ref[A_WIDTH + B_WIDTH:, :], preferred_element_type=F32)
    xn = x_ref[0] + g1_ref[0] * y
    h = _norm_mod(xn, g_ref[...], sc_ref[0], sh_ref[0]).astype(BF16)
    acc = None
    for f in range(w1_ref.shape[1] // tf):
        a = jnp.dot(h, w1_ref[:, f * tf:(f + 1) * tf], preferred_element_type=F32)
        a = jnp.maximum(a, 0.0)
        part = jnp.dot((a * a).astype(BF16), w2_ref[f * tf:(f + 1) * tf, :],
                       preferred_element_type=F32)
        acc = part if acc is None else acc + part
    o_ref[0] = xn + g2_ref[0] * acc


def _mix_mlp(x, ya, yb, yc, wo, g1, g, sc, sh, g2, w1, w2, tm, tf):
    bsz, t, d = x.shape
    dff = w1.shape[1]
    row = pl.BlockSpec((1, 1, d), lambda b, i: (b, 0, 0))
    blk = lambda n: pl.BlockSpec((1, tm, n), lambda b, i: (b, i, 0))
    resident = lambda m, n: pl.BlockSpec((m, n), lambda b, i: (0, 0), pipeline_mode=pl.Buffered(1))
    return pl.pallas_call(
        functools.partial(_mix_mlp_kernel, tf=tf),
        grid=(bsz, t // tm),
        in_specs=[
            blk(d), blk(A_WIDTH), blk(B_WIDTH), blk(C_WIDTH),
            resident(d, d),
            row,
            pl.BlockSpec((1, d), lambda b, i: (0, 0)),
            row, row, row,
            resident(d, dff),
            resident(dff, d),
        ],
        out_specs=blk(d),
        out_shape=jax.ShapeDtypeStruct((bsz, t, d), F32),
        compiler_params=_params("arbitrary", "arbitrary"),
        name="mix_mlp",
    )(x, ya, yb, yc, wo, g1, g, sc, sh, g2, w1, w2)


def _alibi_slopes():
    s = jnp.asarray([2.0 ** (-8.0 * (h + 1) / B_HEADS) for h in range(B_HEADS)], F32)
    return jnp.broadcast_to(s[:, None, None], (B_HEADS, 1, V7X_LANES))


def _alibi_key_bias(slopes, t):
    rem = jnp.arange(t, dtype=F32)[None, :, None] * (slopes[:, :, 0:1] * LOG2E)
    pieces = []
    for _ in range(ALIBI_PIECES):
        bits = lax.bitcast_convert_type(rem, jnp.uint32) & jnp.uint32(0xFFFF0000)
        part = lax.bitcast_convert_type(bits, F32)
        pieces.append(part.astype(BF16))
        rem = rem - part
    table = jnp.concatenate(pieces, axis=-1)
    return jnp.pad(table, ((0, 0), (0, 0), (0, V7X_LANES - ALIBI_PIECES)))


def _pad_rows(w, before, after):
    return jnp.pad(w, ((before, after), (0, 0)))


def _layer(x, mod, p, lambda_init, tiles):
    bsz = x.shape[0]
    sh1, sc1, g1, sh2, sc2, g2 = [m.reshape(bsz, 1, D_MODEL) for m in jnp.split(mod, 6, axis=-1)]
    row = lambda a: a.reshape(1, -1)

    proj, y_c = _in_projection_conv(
        x, row(p["norm1_g"]), sc1, sh1, p["w_in"].astype(BF16), p["conv_w"], row(p["conv_b"]),
        row(p["conv_ln_w"]), row(p["conv_ln_b"]), tiles["proj"])
    y_a = _rwkv_mix(
        proj, row(p["tshift_mu"]), row(p["decay_w0"]),
        _pad_rows(p["decay_w2"], 0, ICLR_LORA).astype(BF16), row(p["iclr_a0"]),
        _pad_rows(p["iclr_a2"], DECAY_LORA, 0).astype(BF16), p["gate_g2"].astype(BF16),
        row(p["k_k"]), row(p["k_a"]), row(p["r_k"]), row(p["lnx_w"]), row(p["lnx_b"]),
        tiles["rwkv"], RWKV_ROWS)
    two = lambda a: row(jnp.concatenate([a, a]))
    y_b = _diff_attention(
        proj, two(p["q_norm_g"]), two(p["k_norm_g"]), row(p["subln_g"]),
        row(p["lambda_q1"]), row(p["lambda_k1"]), row(p["lambda_q2"]), row(p["lambda_k2"]),
        _alibi_slopes(), lambda_init, tiles["attn"], ATTN_HEADS)
    return _mix_mlp(x, y_a, y_b, y_c, p["w_out"].astype(BF16), g1, row(p["norm2_g"]), sc2, sh2,
                    g2, p["mlp_w1"].astype(BF16), p["mlp_w2"].astype(BF16), tiles["mlp_tm"],
                    tiles["mlp_tf"])


_LAYER_PARAMS = ("norm1_g", "w_in", "tshift_mu", "decay_w0", "decay_w2", "iclr_a0", "iclr_a2",
                 "gate_g2", "k_k", "k_a", "r_k", "lnx_w", "lnx_b", "q_norm_g", "k_norm_g",
                 "lambda_q1", "lambda_k1", "lambda_q2", "lambda_k2", "subln_g", "conv_w",
                 "conv_b", "conv_ln_w", "conv_ln_b", "w_out", "norm2_g", "mlp_w1", "mlp_w2")


def _block(x, c, ada_w, ada_b, params, tiles):
    mod = _ada_modulation(c, ada_w, ada_b)
    for i in range(ada_w.shape[0]):
        lambda_init = 0.8 - 0.6 * math.exp(-0.3 * i)
        x = _layer(x, mod[i], {k: v[i] for k, v in params.items()}, lambda_init, tiles)
    return x


def kernel(x, c, ada_w, ada_b, norm1_g, w_in, tshift_mu, decay_w0, decay_w2, iclr_a0, iclr_a2,
           gate_g2, k_k, k_a, r_k, lnx_w, lnx_b, q_norm_g, k_norm_g, lambda_q1, lambda_k1,
           lambda_q2, lambda_k2, subln_g, conv_w, conv_b, conv_ln_w, conv_ln_b, w_out, norm2_g,
           mlp_w1, mlp_w2):
    values = (norm1_g, w_in, tshift_mu, decay_w0, decay_w2, iclr_a0, iclr_a2, gate_g2, k_k, k_a,
              r_k.reshape(r_k.shape[0], -1), lnx_w, lnx_b, q_norm_g, k_norm_g, lambda_q1,
              lambda_k1, lambda_q2, lambda_k2, subln_g, conv_w, conv_b, conv_ln_w, conv_ln_b,
              w_out, norm2_g, mlp_w1, mlp_w2)
    t = x.shape[1]
    tiles = dict(proj=min(PROJ_TM, t), rwkv=min(RWKV_TILE, t), attn=min(ATTN_TQ, t),
                 mlp_tm=min(MLP_TM, t), mlp_tf=MLP_TF)
    return _block(x, c, ada_w, ada_b, dict(zip(_LAYER_PARAMS, values)), tiles)
```

```python
import functools
import math

import jax
import jax.numpy as jnp
from jax import lax
from jax.experimental import pallas as pl
from jax.experimental.pallas import tpu as pltpu

F32 = jnp.float32
BF16 = jnp.bfloat16

D_MODEL = 1024
DEPTH = 2
CHUNK = 64
A_HEADS = 4
A_HEAD_DIM = 64
A_WIDTH = A_HEADS * A_HEAD_DIM
DECAY_LORA = 64
ICLR_LORA = 64
GATE_LORA = 128
A_COLS = 3 * A_WIDTH + DECAY_LORA + ICLR_LORA + GATE_LORA
B_HEADS = 4
B_QK_DIM = 64
B_V_DIM = 2 * B_QK_DIM
B_WIDTH = B_HEADS * B_V_DIM
B_QK_COLS = B_HEADS * 2 * B_QK_DIM
B_COLS = 2 * B_QK_COLS + B_WIDTH
C_WIDTH = D_MODEL - A_WIDTH - B_WIDTH
CONV_WIDTH = 31
C_COLS = 2 * C_WIDTH
N_IN = A_COLS + B_COLS + C_COLS
D_FF = 4 * D_MODEL
RMS_EPS = 1e-6
LN_EPS = 1e-5
LNX_EPS = 64e-5
NEG_INF = -1e30
EXP_NEG_HALF = math.exp(-0.5)

V7X_LANES = 128
V7X_SUBLANES = 8
V7X_VMEM_LIMIT_BYTES = 56 * 1024 * 1024

RWKV_CHUNK = 64
RWKV_TILE = 256
RWKV_ROWS = 4
ATTN_TQ = 512
ATTN_HEADS = 4
CONV_HALO = 32
PROJ_TM = 1024
MLP_TM = 512
MLP_TF = 2048


def _params(*sem):
    return pltpu.CompilerParams(dimension_semantics=sem,
                                vmem_limit_bytes=V7X_VMEM_LIMIT_BYTES)


def _dot(a, b):
    return jnp.dot(a.astype(BF16), b.astype(BF16), preferred_element_type=F32)


def _dot_nt(a, b):
    return lax.dot_general(a.astype(BF16), b.astype(BF16), (((1,), (1,)), ((), ())),
                           preferred_element_type=F32)


def _dot_tn(a, b):
    return lax.dot_general(a.astype(BF16), b.astype(BF16), (((0,), (0,)), ((), ())),
                           preferred_element_type=F32)


def _split_dot(a, b_exact, parts):
    out = None
    rem = a
    for _ in range(parts):
        piece = rem.astype(BF16)
        term = jnp.dot(piece, b_exact, preferred_element_type=F32)
        out = term if out is None else out + term
        rem = rem - piece.astype(F32)
    return out


def _sigmoid(x):
    return 1.0 / (1.0 + jnp.exp(-x))


def _ada_kernel(c_ref, w_ref, b_ref, o_ref):
    c = c_ref[...]
    cond = c * _sigmoid(c)
    o_ref[0] = _dot(cond, w_ref[0]) + b_ref[0]


def _ada_modulation(c, ada_w, ada_b):
    depth, d, n = ada_w.shape
    bsz = c.shape[0]
    tn = 1536
    return pl.pallas_call(
        _ada_kernel,
        grid=(depth, n // tn),
        in_specs=[
            pl.BlockSpec((bsz, d), lambda l, j: (0, 0)),
            pl.BlockSpec((1, d, tn), lambda l, j: (l, 0, j)),
            pl.BlockSpec((1, 1, tn), lambda l, j: (l, 0, j)),
        ],
        out_specs=pl.BlockSpec((1, bsz, tn), lambda l, j: (l, 0, j)),
        out_shape=jax.ShapeDtypeStruct((depth, bsz, n), F32),
        compiler_params=_params("arbitrary", "arbitrary"),
        name="ada_modulation",
    )(c, ada_w, ada_b.reshape(depth, 1, n))


def _norm_mod(x, g, sc, sh):
    ms = jnp.mean(x * x, axis=-1, keepdims=True)
    y = x * lax.rsqrt(ms + RMS_EPS) * g
    return y * (1.0 + sc) + sh


def _conv_module(pc, w_ref, b_ref, lw_ref, lb_ref, o_ref, hbuf_ref, tile, rows):
    @pl.when(pl.program_id(1) == 0)
    def _():
        hbuf_ref[0, 0:CONV_HALO, :] = jnp.zeros((CONV_HALO, C_WIDTH), F32)

    n = CONV_HALO + tile
    hbuf_ref[0, CONV_HALO:n, :] = pc[:, :C_WIDTH] * _sigmoid(pc[:, C_WIDTH:])
    full = hbuf_ref[0]
    for s in range(1, V7X_SUBLANES):
        hbuf_ref[s] = pltpu.roll(full, n - s, 0)
    lead = CONV_HALO - (CONV_WIDTH - 1)
    for c in range(tile // rows):
        acc = jnp.zeros((rows, C_WIDTH), F32)
        for j in range(CONV_WIDTH):
            start = c * rows + lead + j
            s = start % V7X_SUBLANES
            acc = acc + w_ref[j:j + 1, :] * hbuf_ref[s, start - s:start - s + rows, :]
        acc = acc + b_ref[...]
        mu = jnp.mean(acc, axis=-1, keepdims=True)
        d = acc - mu
        var = jnp.mean(d * d, axis=-1, keepdims=True)
        y = d * lax.rsqrt(var + LN_EPS) * lw_ref[...] + lb_ref[...]
        o_ref[0, c * rows:(c + 1) * rows, :] = (y * _sigmoid(y)).astype(o_ref.dtype)
    hbuf_ref[0, 0:CONV_HALO, :] = hbuf_ref[0, tile:n, :]


def _inproj_kernel(x_ref, g_ref, sc_ref, sh_ref, w_ref, cw_ref, cb_ref, clw_ref, clb_ref,
                   o_ref, yc_ref, hbuf_ref, *, tn, rows):
    tile = x_ref.shape[1]
    h = _norm_mod(x_ref[0], g_ref[...], sc_ref[0], sh_ref[0]).astype(BF16)
    c0 = A_COLS + B_COLS
    pc = jnp.dot(h, w_ref[:, c0:], preferred_element_type=F32)
    o_ref[0, :, c0:] = pc
    _conv_module(pc, cw_ref, cb_ref, clw_ref, clb_ref, yc_ref, hbuf_ref, tile, rows)
    for j in range(c0 // tn):
        o_ref[0, :, j * tn:(j + 1) * tn] = jnp.dot(
            h, w_ref[:, j * tn:(j + 1) * tn], preferred_element_type=F32)


def _in_projection_conv(x, g, sc, sh, w, layer, conv_w, conv_b, ln_w, ln_b, tm):
    bsz, t, d = x.shape
    n = w.shape[2]
    vec = lambda k: pl.BlockSpec((1, k), lambda b, i: (0, 0))
    row = pl.BlockSpec((1, 1, d), lambda b, i: (b, 0, 0))
    return pl.pallas_call(
        functools.partial(_inproj_kernel, tn=C_COLS, rows=64),
        grid=(bsz, t // tm),
        in_specs=[
            pl.BlockSpec((1, tm, d), lambda b, i: (b, i, 0)),
            vec(d), row, row,
            pl.BlockSpec((None, d, n), lambda b, i: (layer, 0, 0), pipeline_mode=pl.Buffered(1)),
            pl.BlockSpec((CONV_WIDTH, C_WIDTH), lambda b, i: (0, 0)),
            vec(C_WIDTH), vec(C_WIDTH), vec(C_WIDTH),
        ],
        out_specs=[pl.BlockSpec((1, tm, n), lambda b, i: (b, i, 0)),
                   pl.BlockSpec((1, tm, C_WIDTH), lambda b, i: (b, i, 0))],
        out_shape=[jax.ShapeDtypeStruct((bsz, t, n), F32),
                   jax.ShapeDtypeStruct((bsz, t, C_WIDTH), BF16)],
        scratch_shapes=[pltpu.VMEM((V7X_SUBLANES, CONV_HALO + tm, C_WIDTH), F32)],
        compiler_params=_params("arbitrary", "arbitrary"),
        name="in_projection_conv",
    )(x, g, sc, sh, w, conv_w, conv_b, ln_w, ln_b)


def _rwkv_kernel(pa_ref, mu_ref, w0_ref, w2_ref, a0_ref, a2_ref, g2_ref, kk_ref, ka_ref,
                 rk_ref, lw_ref, lb_ref, ho_ref, tri_ref, o_ref, carry_ref, s_ref, *, rows, tile,
                 chunk):
    aw = A_WIDTH
    hd = A_HEAD_DIM
    n_chunks = tile // chunk

    @pl.when(pl.program_id(1) == 0)
    def _():
        carry_ref[...] = jnp.zeros_like(carry_ref)
        s_ref[...] = jnp.zeros_like(s_ref)

    ri = lax.broadcasted_iota(jnp.int32, (aw, aw), 0)
    ci = lax.broadcasted_iota(jnp.int32, (aw, aw), 1)
    same_head = (ri // hd) == (ci // hd)
    head_ones = ho_ref[...]
    eye_full = ri == ci
    lane_head = lax.broadcasted_iota(jnp.int32, (hd, aw), 1) // hd
    t_c = lax.broadcasted_iota(jnp.int32, (chunk, aw), 0)
    s_c = lax.broadcasted_iota(jnp.int32, (chunk, aw), 1) % chunk
    strict = s_c < t_c
    incl = s_c <= t_c
    eye_c = (s_c == t_c).astype(F32)
    tri = tri_ref[...]
    row = lax.broadcasted_iota(jnp.int32, (tile, 1), 0)

    def head_sum(x, parts=1):
        return _split_dot(x, head_ones, parts)

    def blockdiag(x):
        return jnp.where(same_head, jnp.concatenate([x] * A_HEADS, axis=0), 0.0)

    seq = []
    for bi in range(rows):
        pa = pa_ref[bi]
        crow = bi * V7X_SUBLANES
        prev = jnp.where(row == 0, carry_ref[crow:crow + 1, :], pltpu.roll(pa, 1, 0))
        carry_ref[crow:crow + 1, :] = pa[tile - 1:tile, :]
        xs = pa + mu_ref[...] * (prev - pa)
        r = xs[:, 0:aw]
        k = xs[:, aw:2 * aw]
        v = xs[:, 2 * aw:3 * aw]
        lora = xs[:, 3 * aw:3 * aw + DECAY_LORA + ICLR_LORA]
        dg = xs[:, 3 * aw + DECAY_LORA + ICLR_LORA:]
        u = w0_ref[...] + _dot(jnp.tanh(lora), w2_ref[...])
        logdecay = -EXP_NEG_HALF * _sigmoid(u)
        iclr = _sigmoid(a0_ref[...] + _dot(lora, a2_ref[...]))
        gate = _dot(_sigmoid(dg), g2_ref[...])
        kk = k * kk_ref[...]
        kk = kk * lax.rsqrt(jnp.maximum(head_sum(kk * kk, 2), 1e-24))
        k = k * (1.0 + (iclr - 1.0) * ka_ref[...])
        b = kk * iclr
        cum = None
        rem = logdecay
        for _ in range(3):
            piece = rem.astype(BF16)
            term = jnp.dot(tri, piece, preferred_element_type=F32)
            cum = term if cum is None else cum + term
            rem = rem - piece.astype(F32)
        seq.append(dict(r=r, k=k, v=v, kk=kk, b=b, ld=logdecay, cum=cum, gate=gate))

    chains = [(bi, c) for bi in range(rows) for c in range(n_chunks)]
    ch = []
    for bi, c in chains:
        q = seq[bi]
        sl = slice(c * chunk, (c + 1) * chunk)
        cum_c = q["cum"][sl]
        total = cum_c[chunk - 1:chunk, :]
        w_inv = jnp.exp(-cum_c)
        w_end = jnp.exp(total - cum_c)
        ch.append(dict(
            rt=q["r"][sl] * jnp.exp(cum_c), at=-q["kk"][sl] * jnp.exp(cum_c - q["ld"][sl]),
            bt=q["b"][sl] * w_inv, kt=q["k"][sl] * w_inv, be=q["b"][sl] * w_end,
            ke=q["k"][sl] * w_end, vc=q["v"][sl], total=total))
    for d in ch:
        lhs = jnp.concatenate([d["at"], d["rt"]], axis=0)
        rhs = jnp.concatenate([blockdiag(d["bt"]), blockdiag(d["kt"])], axis=0)
        prod = _dot_nt(lhs, rhs)
        d["low"] = jnp.where(strict, prod[:chunk, :aw], 0.0)
        d["a_ak"] = jnp.where(strict, prod[:chunk, aw:], 0.0)
        d["a_rb"] = jnp.where(incl, prod[chunk:, :aw], 0.0)
        d["a_rk"] = jnp.where(incl, prod[chunk:, aw:], 0.0)
    for d in ch:
        d["pw"] = _dot(d["low"], blockdiag(d["low"]))
        d["ti"] = eye_c + d["low"]
        d["akv"] = _dot(d["a_ak"], blockdiag(d["vc"]))
        d["y_in"] = _dot(d["a_rk"], blockdiag(d["vc"]))
    for _ in range(int(math.log2(chunk)) - 2):
        for d in ch:
            both = _dot(jnp.concatenate([d["pw"], d["ti"]], axis=0), blockdiag(d["pw"]))
            d["pw"] = both[:chunk]
            d["ti"] = d["ti"] + both[chunk:]
    for d in ch:
        d["ti"] = d["ti"] + _dot(d["ti"], blockdiag(d["pw"]))
    for d in ch:
        both = _dot(d["ti"], jnp.concatenate([blockdiag(d["at"]), blockdiag(d["akv"])], axis=1))
        d["ah"] = both[:, :aw]
        d["vh"] = both[:, aw:]
    for d in ch:
        both = _dot(d["a_rb"], jnp.concatenate([blockdiag(d["ah"]), blockdiag(d["vh"])], axis=1))
        d["rh"] = d["rt"] + both[:, :aw]
        d["y_in"] = d["y_in"] + both[:, aw:]
        trans = jnp.where(same_head, _dot_tn(d["ah"], d["be"]), 0.0)
        d["trans"] = trans + jnp.where(eye_full, jnp.exp(d["total"]), 0.0)
        add_full = _dot_tn(jnp.concatenate([d["vh"], d["vc"]], axis=0),
                           jnp.concatenate([d["be"], d["ke"]], axis=0))
        add = None
        for h in range(A_HEADS):
            blk = jnp.where(lane_head == h, add_full[h * hd:(h + 1) * hd], 0.0)
            add = blk if add is None else add + blk
        d["add"] = add

    states = [s_ref[bi * hd:(bi + 1) * hd, :] for bi in range(rows)]
    ys = [[] for _ in range(rows)]
    for c in range(n_chunks):
        for bi in range(rows):
            d = ch[bi * n_chunks + c]
            ys[bi].append(_dot_nt(d["rh"], blockdiag(states[bi])) + d["y_in"])
            states[bi] = _dot(states[bi], d["trans"]) + d["add"]

    inv_n = 1.0 / hd
    for bi in range(rows):
        q = seq[bi]
        s_ref[bi * hd:(bi + 1) * hd, :] = states[bi]
        y = jnp.concatenate(ys[bi], axis=0)
        mu_h = head_sum(y) * inv_n
        yc = y - mu_h
        var_h = head_sum(yc * yc) * inv_n
        yn = yc * lax.rsqrt(var_h + LNX_EPS) * lw_ref[...] + lb_ref[...]
        bonus = head_sum(q["r"] * q["k"] * rk_ref[...]) * q["v"]
        o_ref[bi] = ((yn + bonus) * q["gate"]).astype(o_ref.dtype)


def _rwkv_mix(proj, mu, w0, w2p, a0, a2p, g2, k_k, k_a, r_k, lnx_w, lnx_b, tile, rows):
    bsz, t, _ = proj.shape
    rows = math.gcd(rows, bsz)
    vec = lambda n: pl.BlockSpec((1, n), lambda b, i: (0, 0))
    mat = lambda m, n: pl.BlockSpec((m, n), lambda b, i: (0, 0))
    lane_head = jnp.arange(A_WIDTH) // A_HEAD_DIM
    head_ones = (lane_head[:, None] == lane_head[None, :]).astype(BF16)
    pos = jnp.arange(tile)
    tri = ((pos[None, :] <= pos[:, None])
           & (pos[None, :] // RWKV_CHUNK == pos[:, None] // RWKV_CHUNK)).astype(BF16)
    return pl.pallas_call(
        functools.partial(_rwkv_kernel, rows=rows, tile=tile, chunk=RWKV_CHUNK),
        grid=(bsz // rows, t // tile),
        in_specs=[
            pl.BlockSpec((rows, tile, A_COLS), lambda b, i: (b, i, 0)),
            vec(A_COLS), vec(A_WIDTH), mat(DECAY_LORA + ICLR_LORA, A_WIDTH),
            vec(A_WIDTH), mat(DECAY_LORA + ICLR_LORA, A_WIDTH), mat(GATE_LORA, A_WIDTH),
            vec(A_WIDTH), vec(A_WIDTH), vec(A_WIDTH), vec(A_WIDTH), vec(A_WIDTH),
            mat(A_WIDTH, A_WIDTH), mat(tile, tile),
        ],
        out_specs=pl.BlockSpec((rows, tile, A_WIDTH), lambda b, i: (b, i, 0)),
        out_shape=jax.ShapeDtypeStruct((bsz, t, A_WIDTH), BF16),
        scratch_shapes=[pltpu.VMEM((rows * V7X_SUBLANES, A_COLS), F32),
                        pltpu.VMEM((rows * A_HEAD_DIM, A_WIDTH), F32)],
        compiler_params=_params("arbitrary", "arbitrary"),
        name="rwkv7_mix",
    )(proj, mu, w0, w2p, a0, a2p, g2, k_k, k_a, r_k, lnx_w, lnx_b, head_ones, tri)


LOG2E = math.log2(math.e)
ALIBI_PIECES = 3


def _half_rms(x, gain, scale):
    ri = lax.broadcasted_iota(jnp.int32, (x.shape[1], x.shape[1]), 0)
    ci = lax.broadcasted_iota(jnp.int32, (x.shape[1], x.shape[1]), 1)
    half_ones = ((ri // B_QK_DIM) == (ci // B_QK_DIM)).astype(F32).astype(BF16)
    ms = _split_dot(x * x, half_ones, 1) * (1.0 / B_QK_DIM)
    return x * lax.rsqrt(ms + RMS_EPS) * (gain * scale)


def _lane_fold(x, op):
    parts = [x[:, k * V7X_LANES:(k + 1) * V7X_LANES] for k in range(x.shape[1] // V7X_LANES)]
    while len(parts) > 1:
        pairs = [op(parts[a], parts[a + 1]) for a in range(0, len(parts) - 1, 2)]
        parts = pairs + parts[len(parts) - len(parts) % 2:]
    return parts[0]


def _attn_kernel(q_ref, k_ref, v_ref, qg_ref, kg_ref, sg_ref, lq1_ref, lk1_ref, lq2_ref,
                 lk2_ref, slope_ref, bias_ref, o_ref, kx_ref, vb_ref, corr_ref, *, tq, heads,
                 lambda_init):
    i = pl.program_id(2)
    t_all = k_ref.shape[1]
    half = tq // 2
    hw = B_V_DIM
    hsl = [slice(h * hw, (h + 1) * hw) for h in range(heads)]

    @pl.when(i == 0)
    def _():
        r = lax.broadcasted_iota(jnp.int32, (half, half), 0)
        c = lax.broadcasted_iota(jnp.int32, (half, half), 1)
        for h in range(heads):
            slope2 = slope_ref[h][:, 0:1] * LOG2E
            kx_ref[h, :, 0:hw] = _half_rms(k_ref[0, :, hsl[h]], kg_ref[...], 1.0).astype(BF16)
            kx_ref[h, :, hw:] = bias_ref[h]
            vb_ref[h] = v_ref[0, :, hsl[h]].astype(BF16)
            after = jnp.where(c > r, (c - r).astype(F32) * (-2.0 * slope2), 0.0)
            corr_ref[h] = jnp.where((c // CHUNK) <= (r // CHUNK), after, NEG_INF)

    lane = lax.broadcasted_iota(jnp.int32, (tq, V7X_LANES), 1)
    lo = lane < B_QK_DIM
    ones = (lane < ALIBI_PIECES).astype(F32)
    qxs = []
    for h in range(heads):
        qn = _half_rms(q_ref[0, :, hsl[h]], qg_ref[...], LOG2E * B_QK_DIM ** -0.5)
        q0 = jnp.concatenate([jnp.where(lo, qn, 0.0), ones], axis=1)
        q1 = jnp.concatenate([jnp.where(lo, 0.0, qn), ones], axis=1)
        qxs.append(jnp.concatenate([q0[:half], q1[:half], q0[half:], q1[half:]],
                                   axis=0).astype(BF16))

    lam = (jnp.exp(jnp.sum(lq1_ref[...] * lk1_ref[...], axis=-1, keepdims=True))
           - jnp.exp(jnp.sum(lq2_ref[...] * lk2_ref[...], axis=-1, keepdims=True))
           + lambda_init)

    def scores(h, lhs, start, size):
        return lax.dot_general(lhs, kx_ref[h, start:start + size, :], (((1,), (1,)), ((), ())),
                               preferred_element_type=F32)

    def update(h, s, carry, start, size):
        m, l, acc = carry
        m_new = jnp.maximum(m, jnp.max(s, axis=-1, keepdims=True))
        alpha = jnp.exp2(m - m_new)
        p = jnp.exp2(s - m_new)
        l = alpha * l + jnp.sum(p, axis=-1, keepdims=True)
        pv = jnp.dot(p.astype(BF16), vb_ref[h, start:start + size, :], preferred_element_type=F32)
        return m_new, l, alpha * acc + pv

    def attend(n):
        d0 = n * tq
        carry = [(jnp.full((2 * tq, 1), NEG_INF, F32), jnp.zeros((2 * tq, 1), F32),
                  jnp.zeros((2 * tq, B_V_DIM), F32)) for _ in range(heads)]
        for j in range(n):
            for h in range(heads):
                carry[h] = update(h, scores(h, qxs[h], j * tq, tq), carry[h], j * tq, tq)
        corr2 = [jnp.concatenate([corr_ref[h], corr_ref[h]], axis=0) for h in range(heads)]
        for h in range(heads):
            s1 = scores(h, qxs[h], d0, half)
            s1 = jnp.concatenate([s1[:tq] + corr2[h], s1[tq:]], axis=0)
            carry[h] = update(h, s1, carry[h], d0, half)
        outs = []
        for h in range(heads):
            m, l, acc = carry[h]
            s2 = scores(h, qxs[h][tq:], d0 + half, half) + corr2[h]
            _, l_b, acc_b = update(h, s2, (m[tq:], l[tq:], acc[tq:]), d0 + half, half)
            o_a = acc[:tq] / l[:tq]
            o_b = acc_b / l_b
            d = jnp.concatenate([o_a[:half] - lam * o_a[half:], o_b[:half] - lam * o_b[half:]],
                                axis=0)
            ms = jnp.mean(d * d, axis=-1, keepdims=True)
            outs.append(d * lax.rsqrt(ms + RMS_EPS) * sg_ref[...] * (1.0 - lambda_init))
        o_ref[0] = jnp.concatenate(outs, axis=1).astype(o_ref.dtype)

    for n in range(t_all // tq):
        pl.when(i == n)(functools.partial(attend, n))


def _diff_attention(proj, qg2, kg2, sg, lq1, lk1, lq2, lk2, slopes, lambda_init, tq, heads):
    bsz, t, _ = proj.shape
    width = heads * B_V_DIM
    q_blk = A_COLS // width
    k_blk = (A_COLS + B_QK_COLS) // width
    v_blk = (A_COLS + 2 * B_QK_COLS) // width
    vec = lambda n: pl.BlockSpec((1, n), lambda b, h, i: (0, 0))
    return pl.pallas_call(
        functools.partial(_attn_kernel, tq=tq, heads=heads, lambda_init=lambda_init),
        grid=(bsz, B_HEADS // heads, t // tq),
        in_specs=[
            pl.BlockSpec((1, tq, width), lambda b, h, i: (b, i, q_blk + h)),
            pl.BlockSpec((1, t, width), lambda b, h, i: (b, 0, k_blk + h)),
            pl.BlockSpec((1, t, width), lambda b, h, i: (b, 0, v_blk + h)),
            vec(B_V_DIM), vec(B_V_DIM), vec(B_V_DIM),
            vec(B_QK_DIM), vec(B_QK_DIM), vec(B_QK_DIM), vec(B_QK_DIM),
            pl.BlockSpec((heads, 1, V7X_LANES), lambda b, h, i: (h, 0, 0)),
            pl.BlockSpec((heads, t, V7X_LANES), lambda b, h, i: (h, 0, 0)),
        ],
        out_specs=pl.BlockSpec((1, tq, width), lambda b, h, i: (b, i, h)),
        out_shape=jax.ShapeDtypeStruct((bsz, t, B_WIDTH), BF16),
        scratch_shapes=[pltpu.VMEM((heads, t, B_V_DIM + V7X_LANES), BF16),
                        pltpu.VMEM((heads, t, B_V_DIM), BF16),
                        pltpu.VMEM((heads, tq // 2, tq // 2), F32)],
        compiler_params=_params("arbitrary", "arbitrary", "arbitrary"),
        name="diff_attention",
    )(proj, proj, proj, qg2, kg2, sg, lq1, lk1, lq2, lk2, slopes, _alibi_key_bias(slopes, t))


def _mix_mlp_kernel(x_ref, ya_ref, yb_ref, yc_ref, wo_ref, g1_ref, g_ref, sc_ref, sh_ref, g2_ref,
                    w1_ref, w2_ref, o_ref, *, tf):
    y = jnp.dot(ya_ref[0], wo_ref[0:A_WIDTH, :], preferred_element_type=F32)
    y = y + jnp.dot(yb_ref[0], wo_ref[A_WIDTH:A_WIDTH + B_WIDTH, :], preferred_element_type=F32)
    y = y + jnp.dot(yc_ref[0], wo_ref[A_WIDTH + B_WIDTH:, :], preferred_element_type=F32)
    xn = x_ref[0] + g1_ref[0] * y
    h = _norm_mod(xn, g_ref[...], sc_ref[0], sh_ref[0]).astype(BF16)
    acc = None
    for f in range(w1_ref.shape[1] // tf):
        a = jnp.dot(h, w1_ref[:, f * tf:(f + 1) * tf], preferred_element_type=F32)
        a = jnp.maximum(a, 0.0)
        part = jnp.dot((a * a).astype(BF16), w2_ref[f * tf:(f + 1) * tf, :],
                       preferred_element_type=F32)
        acc = part if acc is None else acc + part
    o_ref[0] = xn + g2_ref[0] * acc


def _mix_mlp(x, ya, yb, yc, wo, g1, g, sc, sh, g2, w1, w2, layer, tm, tf):
    bsz, t, d = x.shape
    dff = w1.shape[2]
    row = pl.BlockSpec((1, 1, d), lambda b, i: (b, 0, 0))
    blk = lambda n: pl.BlockSpec((1, tm, n), lambda b, i: (b, i, 0))
    resident = lambda m, n: pl.BlockSpec((None, m, n), lambda b, i: (layer, 0, 0),
                                         pipeline_mode=pl.Buffered(1))
    return pl.pallas_call(
        functools.partial(_mix_mlp_kernel, tf=tf),
        grid=(bsz, t // tm),
        in_specs=[
            blk(d), blk(A_WIDTH), blk(B_WIDTH), blk(C_WIDTH),
            resident(d, d),
            row,
            pl.BlockSpec((1, d), lambda b, i: (0, 0)),
            row, row, row,
            resident(d, dff),
            resident(dff, d),
        ],
        out_specs=blk(d),
        out_shape=jax.ShapeDtypeStruct((bsz, t, d), F32),
        compiler_params=_params("arbitrary", "arbitrary"),
        name="mix_mlp",
    )(x, ya, yb, yc, wo, g1, g, sc, sh, g2, w1, w2)


def _alibi_slopes():
    s = jnp.asarray([2.0 ** (-8.0 * (h + 1) / B_HEADS) for h in range(B_HEADS)], F32)
    return jnp.broadcast_to(s[:, None, None], (B_HEADS, 1, V7X_LANES))


def _alibi_key_bias(slopes, t):
    rem = jnp.arange(t, dtype=F32)[None, :, None] * (slopes[:, :, 0:1] * LOG2E)
    pieces = []
    for _ in range(ALIBI_PIECES):
        bits = lax.bitcast_convert_type(rem, jnp.uint32) & jnp.uint32(0xFFFF0000)
        part = lax.bitcast_convert_type(bits, F32)
        pieces.append(part.astype(BF16))
        rem = rem - part
    table = jnp.concatenate(pieces, axis=-1)
    return jnp.pad(table, ((0, 0), (0, 0), (0, V7X_LANES - ALIBI_PIECES)))


def _pad_rows(w, before, after):
    return jnp.pad(w, ((before, after), (0, 0)))


def _layer(x, mod, p, big, layer, lambda_init, tiles):
    bsz = x.shape[0]
    sh1, sc1, g1, sh2, sc2, g2 = [m.reshape(bsz, 1, D_MODEL) for m in jnp.split(mod, 6, axis=-1)]
    row = lambda a: a.reshape(1, -1)

    proj, y_c = _in_projection_conv(
        x, row(p["norm1_g"]), sc1, sh1, big["w_in"], layer, p["conv_w"], row(p["conv_b"]),
        row(p["conv_ln_w"]), row(p["conv_ln_b"]), tiles["proj"])
    y_a = _rwkv_mix(
        proj, row(p["tshift_mu"]), row(p["decay_w0"]),
        _pad_rows(p["decay_w2"], 0, ICLR_LORA).astype(BF16), row(p["iclr_a0"]),
        _pad_rows(p["iclr_a2"], DECAY_LORA, 0).astype(BF16), p["gate_g2"].astype(BF16),
        row(p["k_k"]), row(p["k_a"]), row(p["r_k"]), row(p["lnx_w"]), row(p["lnx_b"]),
        tiles["rwkv"], RWKV_ROWS)
    two = lambda a: row(jnp.concatenate([a, a]))
    y_b = _diff_attention(
        proj, two(p["q_norm_g"]), two(p["k_norm_g"]), row(p["subln_g"]),
        row(p["lambda_q1"]), row(p["lambda_k1"]), row(p["lambda_q2"]), row(p["lambda_k2"]),
        _alibi_slopes(), lambda_init, tiles["attn"], ATTN_HEADS)
    return _mix_mlp(x, y_a, y_b, y_c, big["w_out"], g1, row(p["norm2_g"]), sc2, sh2, g2,
                    big["mlp_w1"], big["mlp_w2"], layer, tiles["mlp_tm"], tiles["mlp_tf"])


_LAYER_PARAMS = ("norm1_g", "w_in", "tshift_mu", "decay_w0", "decay_w2", "iclr_a0", "iclr_a2",
                 "gate_g2", "k_k", "k_a", "r_k", "lnx_w", "lnx_b", "q_norm_g", "k_norm_g",
                 "lambda_q1", "lambda_k1", "lambda_q2", "lambda_k2", "subln_g", "conv_w",
                 "conv_b", "conv_ln_w", "conv_ln_b", "w_out", "norm2_g", "mlp_w1", "mlp_w2")


_STACKED_MATRICES = ("w_in", "w_out", "mlp_w1", "mlp_w2")


def _block(x, c, ada_w, ada_b, params, tiles):
    mod = _ada_modulation(c, ada_w, ada_b)
    big = {k: params[k].astype(BF16) for k in _STACKED_MATRICES}
    for i in range(ada_w.shape[0]):
        lambda_init = 0.8 - 0.6 * math.exp(-0.3 * i)
        small = {k: v[i] for k, v in params.items() if k not in _STACKED_MATRICES}
        x = _layer(x, mod[i], small, big, i, lambda_init, tiles)
    return x


def kernel(x, c, ada_w, ada_b, norm1_g, w_in, tshift_mu, decay_w0, decay_w2, iclr_a0, iclr_a2,
           gate_g2, k_k, k_a, r_k, lnx_w, lnx_b, q_norm_g, k_norm_g, lambda_q1, lambda_k1,
           lambda_q2, lambda_k2, subln_g, conv_w, conv_b, conv_ln_w, conv_ln_b, w_out, norm2_g,
           mlp_w1, mlp_w2):
    values = (norm1_g, w_in, tshift_mu, decay_w0, decay_w2, iclr_a0, iclr_a2, gate_g2, k_k, k_a,
              r_k.reshape(r_k.shape[0], -1), lnx_w, lnx_b, q_norm_g, k_norm_g, lambda_q1,
              lambda_k1, lambda_q2, lambda_k2, subln_g, conv_w, conv_b, conv_ln_w, conv_ln_b,
              w_out, norm2_g, mlp_w1, mlp_w2)
    t = x.shape[1]
    tiles = dict(proj=min(PROJ_TM, t), rwkv=min(RWKV_TILE, t), attn=min(ATTN_TQ, t),
                 mlp_tm=min(MLP_TM, t), mlp_tf=MLP_TF)
    return _block(x, c, ada_w, ada_b, dict(zip(_LAYER_PARAMS, values)), tiles)
```

```python
import functools
import math

import jax
import jax.numpy as jnp
from jax import lax
from jax.experimental import pallas as pl
from jax.experimental.pallas import tpu as pltpu

F32 = jnp.float32
BF16 = jnp.bfloat16

D_MODEL = 1024
DEPTH = 2
CHUNK = 64
A_HEADS = 4
A_HEAD_DIM = 64
A_WIDTH = A_HEADS * A_HEAD_DIM
DECAY_LORA = 64
ICLR_LORA = 64
GATE_LORA = 128
A_COLS = 3 * A_WIDTH + DECAY_LORA + ICLR_LORA + GATE_LORA
B_HEADS = 4
B_QK_DIM = 64
B_V_DIM = 2 * B_QK_DIM
B_WIDTH = B_HEADS * B_V_DIM
B_QK_COLS = B_HEADS * 2 * B_QK_DIM
B_COLS = 2 * B_QK_COLS + B_WIDTH
C_WIDTH = D_MODEL - A_WIDTH - B_WIDTH
CONV_WIDTH = 31
C_COLS = 2 * C_WIDTH
N_IN = A_COLS + B_COLS + C_COLS
D_FF = 4 * D_MODEL
RMS_EPS = 1e-6
LN_EPS = 1e-5
LNX_EPS = 64e-5
NEG_INF = -1e30
EXP_NEG_HALF = math.exp(-0.5)

V7X_LANES = 128
V7X_SUBLANES = 8
V7X_VMEM_LIMIT_BYTES = 56 * 1024 * 1024

RWKV_CHUNK = 64
RWKV_TILE = 256
RWKV_ROWS = 4
ATTN_TQ = 512
ATTN_HEADS = 4
CONV_HALO = 32
PROJ_TM = 1024
MLP_TM = 1024
MLP_TF = 1024


def _params(*sem):
    return pltpu.CompilerParams(dimension_semantics=sem,
                                vmem_limit_bytes=V7X_VMEM_LIMIT_BYTES)


def _dot(a, b):
    return jnp.dot(a.astype(BF16), b.astype(BF16), preferred_element_type=F32)


def _dot_nt(a, b):
    return lax.dot_general(a.astype(BF16), b.astype(BF16), (((1,), (1,)), ((), ())),
                           preferred_element_type=F32)


def _dot_tn(a, b):
    return lax.dot_general(a.astype(BF16), b.astype(BF16), (((0,), (0,)), ((), ())),
                           preferred_element_type=F32)


def _split_dot(a, b_exact, parts):
    out = None
    rem = a
    for _ in range(parts):
        piece = rem.astype(BF16)
        term = jnp.dot(piece, b_exact, preferred_element_type=F32)
        out = term if out is None else out + term
        rem = rem - piece.astype(F32)
    return out


def _sigmoid(x):
    return 1.0 / (1.0 + jnp.exp(-x))


def _ada_kernel(c_ref, w_ref, b_ref, o_ref):
    c = c_ref[...]
    cond = c * _sigmoid(c)
    o_ref[0] = _dot(cond, w_ref[0]) + b_ref[0]


def _ada_modulation(c, ada_w, ada_b):
    depth, d, n = ada_w.shape
    bsz = c.shape[0]
    tn = 1536
    return pl.pallas_call(
        _ada_kernel,
        grid=(depth, n // tn),
        in_specs=[
            pl.BlockSpec((bsz, d), lambda l, j: (0, 0)),
            pl.BlockSpec((1, d, tn), lambda l, j: (l, 0, j)),
            pl.BlockSpec((1, 1, tn), lambda l, j: (l, 0, j)),
        ],
        out_specs=pl.BlockSpec((1, bsz, tn), lambda l, j: (l, 0, j)),
        out_shape=jax.ShapeDtypeStruct((depth, bsz, n), F32),
        compiler_params=_params("arbitrary", "arbitrary"),
        name="ada_modulation",
    )(c, ada_w, ada_b.reshape(depth, 1, n))


def _norm_mod(x, g, sc, sh):
    ms = jnp.mean(x * x, axis=-1, keepdims=True)
    y = x * lax.rsqrt(ms + RMS_EPS) * g
    return y * (1.0 + sc) + sh


def _conv_module(pc, w_ref, b_ref, lw_ref, lb_ref, o_ref, hbuf_ref, tile, rows):
    @pl.when(pl.program_id(1) == 0)
    def _():
        hbuf_ref[0, 0:CONV_HALO, :] = jnp.zeros((CONV_HALO, C_WIDTH), F32)

    n = CONV_HALO + tile
    hbuf_ref[0, CONV_HALO:n, :] = pc[:, :C_WIDTH] * _sigmoid(pc[:, C_WIDTH:])
    full = hbuf_ref[0]
    for s in range(1, V7X_SUBLANES):
        hbuf_ref[s] = pltpu.roll(full, n - s, 0)
    lead = CONV_HALO - (CONV_WIDTH - 1)
    for c in range(tile // rows):
        acc = jnp.zeros((rows, C_WIDTH), F32)
        for j in range(CONV_WIDTH):
            start = c * rows + lead + j
            s = start % V7X_SUBLANES
            acc = acc + w_ref[j:j + 1, :] * hbuf_ref[s, start - s:start - s + rows, :]
        acc = acc + b_ref[...]
        mu = jnp.mean(acc, axis=-1, keepdims=True)
        d = acc - mu
        var = jnp.mean(d * d, axis=-1, keepdims=True)
        y = d * lax.rsqrt(var + LN_EPS) * lw_ref[...] + lb_ref[...]
        o_ref[0, c * rows:(c + 1) * rows, :] = (y * _sigmoid(y)).astype(o_ref.dtype)
    hbuf_ref[0, 0:CONV_HALO, :] = hbuf_ref[0, tile:n, :]


def _inproj_kernel(x_ref, g_ref, sc_ref, sh_ref, w_ref, cw_ref, cb_ref, clw_ref, clb_ref,
                   o_ref, yc_ref, hbuf_ref, *, tn, rows):
    tile = x_ref.shape[1]
    h = _norm_mod(x_ref[0], g_ref[...], sc_ref[0], sh_ref[0]).astype(BF16)
    c0 = A_COLS + B_COLS
    pc = jnp.dot(h, w_ref[:, c0:], preferred_element_type=F32)
    o_ref[0, :, c0:] = pc
    _conv_module(pc, cw_ref, cb_ref, clw_ref, clb_ref, yc_ref, hbuf_ref, tile, rows)
    for j in range(c0 // tn):
        o_ref[0, :, j * tn:(j + 1) * tn] = jnp.dot(
            h, w_ref[:, j * tn:(j + 1) * tn], preferred_element_type=F32)


def _in_projection_conv(x, g, sc, sh, w, layer, conv_w, conv_b, ln_w, ln_b, tm):
    bsz, t, d = x.shape
    n = w.shape[2]
    vec = lambda k: pl.BlockSpec((1, k), lambda b, i: (0, 0))
    row = pl.BlockSpec((1, 1, d), lambda b, i: (b, 0, 0))
    return pl.pallas_call(
        functools.partial(_inproj_kernel, tn=C_COLS, rows=64),
        grid=(bsz, t // tm),
        in_specs=[
            pl.BlockSpec((1, tm, d), lambda b, i: (b, i, 0)),
            vec(d), row, row,
            pl.BlockSpec((None, d, n), lambda b, i: (layer, 0, 0), pipeline_mode=pl.Buffered(1)),
            pl.BlockSpec((CONV_WIDTH, C_WIDTH), lambda b, i: (0, 0)),
            vec(C_WIDTH), vec(C_WIDTH), vec(C_WIDTH),
        ],
        out_specs=[pl.BlockSpec((1, tm, n), lambda b, i: (b, i, 0)),
                   pl.BlockSpec((1, tm, C_WIDTH), lambda b, i: (b, i, 0))],
        out_shape=[jax.ShapeDtypeStruct((bsz, t, n), F32),
                   jax.ShapeDtypeStruct((bsz, t, C_WIDTH), BF16)],
        scratch_shapes=[pltpu.VMEM((V7X_SUBLANES, CONV_HALO + tm, C_WIDTH), F32)],
        compiler_params=_params("arbitrary", "arbitrary"),
        name="in_projection_conv",
    )(x, g, sc, sh, w, conv_w, conv_b, ln_w, ln_b)


def _rwkv_kernel(pa_ref, mu_ref, w0_ref, w2_ref, a0_ref, a2_ref, g2_ref, kk_ref, ka_ref,
                 rk_ref, lw_ref, lb_ref, ho_ref, tri_ref, o_ref, carry_ref, s_ref, *, rows, tile,
                 chunk):
    aw = A_WIDTH
    hd = A_HEAD_DIM
    n_chunks = tile // chunk

    @pl.when(pl.program_id(1) == 0)
    def _():
        carry_ref[...] = jnp.zeros_like(carry_ref)
        s_ref[...] = jnp.zeros_like(s_ref)

    ri = lax.broadcasted_iota(jnp.int32, (aw, aw), 0)
    ci = lax.broadcasted_iota(jnp.int32, (aw, aw), 1)
    same_head = (ri // hd) == (ci // hd)
    head_ones = ho_ref[...]
    eye_full = ri == ci
    lane_head = lax.broadcasted_iota(jnp.int32, (hd, aw), 1) // hd
    t_c = lax.broadcasted_iota(jnp.int32, (chunk, aw), 0)
    s_c = lax.broadcasted_iota(jnp.int32, (chunk, aw), 1) % chunk
    strict = s_c < t_c
    incl = s_c <= t_c
    eye_c = (s_c == t_c).astype(F32)
    tri = tri_ref[...]
    row = lax.broadcasted_iota(jnp.int32, (tile, 1), 0)

    def head_sum(x, parts=1):
        return _split_dot(x, head_ones, parts)

    def blockdiag(x):
        return jnp.where(same_head, jnp.concatenate([x] * A_HEADS, axis=0), 0.0)

    seq = []
    for bi in range(rows):
        pa = pa_ref[bi]
        crow = bi * V7X_SUBLANES
        prev = jnp.where(row == 0, carry_ref[crow:crow + 1, :], pltpu.roll(pa, 1, 0))
        carry_ref[crow:crow + 1, :] = pa[tile - 1:tile, :]
        xs = pa + mu_ref[...] * (prev - pa)
        r = xs[:, 0:aw]
        k = xs[:, aw:2 * aw]
        v = xs[:, 2 * aw:3 * aw]
        lora = xs[:, 3 * aw:3 * aw + DECAY_LORA + ICLR_LORA]
        dg = xs[:, 3 * aw + DECAY_LORA + ICLR_LORA:]
        u = w0_ref[...] + _dot(jnp.tanh(lora), w2_ref[...])
        logdecay = -EXP_NEG_HALF * _sigmoid(u)
        iclr = _sigmoid(a0_ref[...] + _dot(lora, a2_ref[...]))
        gate = _dot(_sigmoid(dg), g2_ref[...])
        kk = k * kk_ref[...]
        kk = kk * lax.rsqrt(jnp.maximum(head_sum(kk * kk, 2), 1e-24))
        k = k * (1.0 + (iclr - 1.0) * ka_ref[...])
        b = kk * iclr
        cum = None
        rem = logdecay
        for _ in range(3):
            piece = rem.astype(BF16)
            term = jnp.dot(tri, piece, preferred_element_type=F32)
            cum = term if cum is None else cum + term
            rem = rem - piece.astype(F32)
        seq.append(dict(r=r, k=k, v=v, kk=kk, b=b, ld=logdecay, cum=cum, gate=gate))

    chains = [(bi, c) for bi in range(rows) for c in range(n_chunks)]
    ch = []
    for bi, c in chains:
        q = seq[bi]
        sl = slice(c * chunk, (c + 1) * chunk)
        cum_c = q["cum"][sl]
        total = cum_c[chunk - 1:chunk, :]
        w_inv = jnp.exp(-cum_c)
        w_end = jnp.exp(total - cum_c)
        ch.append(dict(
            rt=q["r"][sl] * jnp.exp(cum_c), at=-q["kk"][sl] * jnp.exp(cum_c - q["ld"][sl]),
            bt=q["b"][sl] * w_inv, kt=q["k"][sl] * w_inv, be=q["b"][sl] * w_end,
            ke=q["k"][sl] * w_end, vc=q["v"][sl], total=total))
    for d in ch:
        lhs = jnp.concatenate([d["at"], d["rt"]], axis=0)
        rhs = jnp.concatenate([blockdiag(d["bt"]), blockdiag(d["kt"])], axis=0)
        prod = _dot_nt(lhs, rhs)
        d["low"] = jnp.where(strict, prod[:chunk, :aw], 0.0)
        d["a_ak"] = jnp.where(strict, prod[:chunk, aw:], 0.0)
        d["a_rb"] = jnp.where(incl, prod[chunk:, :aw], 0.0)
        d["a_rk"] = jnp.where(incl, prod[chunk:, aw:], 0.0)
    for d in ch:
        d["pw"] = _dot(d["low"], blockdiag(d["low"]))
        d["ti"] = eye_c + d["low"]
        d["akv"] = _dot(d["a_ak"], blockdiag(d["vc"]))
        d["y_in"] = _dot(d["a_rk"], blockdiag(d["vc"]))
    for _ in range(int(math.log2(chunk)) - 2):
        for d in ch:
            both = _dot(jnp.concatenate([d["pw"], d["ti"]], axis=0), blockdiag(d["pw"]))
            d["pw"] = both[:chunk]
            d["ti"] = d["ti"] + both[chunk:]
    for d in ch:
        d["ti"] = d["ti"] + _dot(d["ti"], blockdiag(d["pw"]))
    for d in ch:
        both = _dot(d["ti"], jnp.concatenate([blockdiag(d["at"]), blockdiag(d["akv"])], axis=1))
        d["ah"] = both[:, :aw]
        d["vh"] = both[:, aw:]
    for d in ch:
        both = _dot(d["a_rb"], jnp.concatenate([blockdiag(d["ah"]), blockdiag(d["vh"])], axis=1))
        d["rh"] = d["rt"] + both[:, :aw]
        d["y_in"] = d["y_in"] + both[:, aw:]
        trans = jnp.where(same_head, _dot_tn(d["ah"], d["be"]), 0.0)
        d["trans"] = trans + jnp.where(eye_full, jnp.exp(d["total"]), 0.0)
        add_full = _dot_tn(jnp.concatenate([d["vh"], d["vc"]], axis=0),
                           jnp.concatenate([d["be"], d["ke"]], axis=0))
        add = None
        for h in range(A_HEADS):
            blk = jnp.where(lane_head == h, add_full[h * hd:(h + 1) * hd], 0.0)
            add = blk if add is None else add + blk
        d["add"] = add

    states = [s_ref[bi * hd:(bi + 1) * hd, :] for bi in range(rows)]
    ys = [[] for _ in range(rows)]
    for c in range(n_chunks):
        for bi in range(rows):
            d = ch[bi * n_chunks + c]
            ys[bi].append(_dot_nt(d["rh"], blockdiag(states[bi])) + d["y_in"])
            states[bi] = _dot(states[bi], d["trans"]) + d["add"]

    inv_n = 1.0 / hd
    for bi in range(rows):
        q = seq[bi]
        s_ref[bi * hd:(bi + 1) * hd, :] = states[bi]
        y = jnp.concatenate(ys[bi], axis=0)
        mu_h = head_sum(y) * inv_n
        yc = y - mu_h
        var_h = head_sum(yc * yc) * inv_n
        yn = yc * lax.rsqrt(var_h + LNX_EPS) * lw_ref[...] + lb_ref[...]
        bonus = head_sum(q["r"] * q["k"] * rk_ref[...]) * q["v"]
        o_ref[bi] = ((yn + bonus) * q["gate"]).astype(o_ref.dtype)


def _rwkv_mix(proj, mu, w0, w2p, a0, a2p, g2, k_k, k_a, r_k, lnx_w, lnx_b, tile, rows):
    bsz, t, _ = proj.shape
    rows = math.gcd(rows, bsz)
    vec = lambda n: pl.BlockSpec((1, n), lambda b, i: (0, 0))
    mat = lambda m, n: pl.BlockSpec((m, n), lambda b, i: (0, 0))
    lane_head = jnp.arange(A_WIDTH) // A_HEAD_DIM
    head_ones = (lane_head[:, None] == lane_head[None, :]).astype(BF16)
    pos = jnp.arange(tile)
    tri = ((pos[None, :] <= pos[:, None])
           & (pos[None, :] // RWKV_CHUNK == pos[:, None] // RWKV_CHUNK)).astype(BF16)
    return pl.pallas_call(
        functools.partial(_rwkv_kernel, rows=rows, tile=tile, chunk=RWKV_CHUNK),
        grid=(bsz // rows, t // tile),
        in_specs=[
            pl.BlockSpec((rows, tile, A_COLS), lambda b, i: (b, i, 0)),
            vec(A_COLS), vec(A_WIDTH), mat(DECAY_LORA + ICLR_LORA, A_WIDTH),
            vec(A_WIDTH), mat(DECAY_LORA + ICLR_LORA, A_WIDTH), mat(GATE_LORA, A_WIDTH),
            vec(A_WIDTH), vec(A_WIDTH), vec(A_WIDTH), vec(A_WIDTH), vec(A_WIDTH),
            mat(A_WIDTH, A_WIDTH), mat(tile, tile),
        ],
        out_specs=pl.BlockSpec((rows, tile, A_WIDTH), lambda b, i: (b, i, 0)),
        out_shape=jax.ShapeDtypeStruct((bsz, t, A_WIDTH), BF16),
        scratch_shapes=[pltpu.VMEM((rows * V7X_SUBLANES, A_COLS), F32),
                        pltpu.VMEM((rows * A_HEAD_DIM, A_WIDTH), F32)],
        compiler_params=_params("arbitrary", "arbitrary"),
        name="rwkv7_mix",
    )(proj, mu, w0, w2p, a0, a2p, g2, k_k, k_a, r_k, lnx_w, lnx_b, head_ones, tri)


LOG2E = math.log2(math.e)
ALIBI_PIECES = 3


def _half_rms(x, gain, scale):
    ri = lax.broadcasted_iota(jnp.int32, (x.shape[1], x.shape[1]), 0)
    ci = lax.broadcasted_iota(jnp.int32, (x.shape[1], x.shape[1]), 1)
    half_ones = ((ri // B_QK_DIM) == (ci // B_QK_DIM)).astype(F32).astype(BF16)
    ms = _split_dot(x * x, half_ones, 1) * (1.0 / B_QK_DIM)
    return x * lax.rsqrt(ms + RMS_EPS) * (gain * scale)


def _lane_fold(x, op):
    parts = [x[:, k * V7X_LANES:(k + 1) * V7X_LANES] for k in range(x.shape[1] // V7X_LANES)]
    while len(parts) > 1:
        pairs = [op(parts[a], parts[a + 1]) for a in range(0, len(parts) - 1, 2)]
        parts = pairs + parts[len(parts) - len(parts) % 2:]
    return parts[0]


def _attn_kernel(q_ref, k_ref, v_ref, qg_ref, kg_ref, sg_ref, lq1_ref, lk1_ref, lq2_ref,
                 lk2_ref, slope_ref, bias_ref, o_ref, kx_ref, vb_ref, corr_ref, *, tq, heads,
                 lambda_init):
    i = pl.program_id(2)
    t_all = k_ref.shape[1]
    half = tq // 2
    hw = B_V_DIM
    hsl = [slice(h * hw, (h + 1) * hw) for h in range(heads)]

    @pl.when(i == 0)
    def _():
        r = lax.broadcasted_iota(jnp.int32, (half, half), 0)
        c = lax.broadcasted_iota(jnp.int32, (half, half), 1)
        for h in range(heads):
            slope2 = slope_ref[h][:, 0:1] * LOG2E
            kx_ref[h, :, 0:hw] = _half_rms(k_ref[0, :, hsl[h]], kg_ref[...], 1.0).astype(BF16)
            kx_ref[h, :, hw:] = bias_ref[h]
            vb_ref[h] = v_ref[0, :, hsl[h]].astype(BF16)
            after = jnp.where(c > r, (c - r).astype(F32) * (-2.0 * slope2), 0.0)
            corr_ref[h] = jnp.where((c // CHUNK) <= (r // CHUNK), after, NEG_INF)

    lane = lax.broadcasted_iota(jnp.int32, (tq, V7X_LANES), 1)
    lo = lane < B_QK_DIM
    ones = (lane < ALIBI_PIECES).astype(F32)
    qxs = []
    for h in range(heads):
        qn = _half_rms(q_ref[0, :, hsl[h]], qg_ref[...], LOG2E * B_QK_DIM ** -0.5)
        q0 = jnp.concatenate([jnp.where(lo, qn, 0.0), ones], axis=1)
        q1 = jnp.concatenate([jnp.where(lo, 0.0, qn), ones], axis=1)
        qxs.append(jnp.concatenate([q0[:half], q1[:half], q0[half:], q1[half:]],
                                   axis=0).astype(BF16))

    lam = (jnp.exp(jnp.sum(lq1_ref[...] * lk1_ref[...], axis=-1, keepdims=True))
           - jnp.exp(jnp.sum(lq2_ref[...] * lk2_ref[...], axis=-1, keepdims=True))
           + lambda_init)

    def scores(h, lhs, start, size):
        return lax.dot_general(lhs, kx_ref[h, start:start + size, :], (((1,), (1,)), ((), ())),
                               preferred_element_type=F32)

    def update(h, s, carry, start, size):
        m, l, acc = carry
        m_new = jnp.maximum(m, jnp.max(s, axis=-1, keepdims=True))
        alpha = jnp.exp2(m - m_new)
        p = jnp.exp2(s - m_new)
        l = alpha * l + jnp.sum(p, axis=-1, keepdims=True)
        pv = jnp.dot(p.astype(BF16), vb_ref[h, start:start + size, :], preferred_element_type=F32)
        return m_new, l, alpha * acc + pv

    def attend(n):
        d0 = n * tq
        carry = [(jnp.full((2 * tq, 1), NEG_INF, F32), jnp.zeros((2 * tq, 1), F32),
                  jnp.zeros((2 * tq, B_V_DIM), F32)) for _ in range(heads)]
        for j in range(n):
            for h in range(heads):
                carry[h] = update(h, scores(h, qxs[h], j * tq, tq), carry[h], j * tq, tq)
        corr2 = [jnp.concatenate([corr_ref[h], corr_ref[h]], axis=0) for h in range(heads)]
        for h in range(heads):
            s1 = scores(h, qxs[h], d0, half)
            s1 = jnp.concatenate([s1[:tq] + corr2[h], s1[tq:]], axis=0)
            carry[h] = update(h, s1, carry[h], d0, half)
        outs = []
        for h in range(heads):
            m, l, acc = carry[h]
            s2 = scores(h, qxs[h][tq:], d0 + half, half) + corr2[h]
            _, l_b, acc_b = update(h, s2, (m[tq:], l[tq:], acc[tq:]), d0 + half, half)
            o_a = acc[:tq] / l[:tq]
            o_b = acc_b / l_b
            d = jnp.concatenate([o_a[:half] - lam * o_a[half:], o_b[:half] - lam * o_b[half:]],
                                axis=0)
            ms = jnp.mean(d * d, axis=-1, keepdims=True)
            outs.append(d * lax.rsqrt(ms + RMS_EPS) * sg_ref[...] * (1.0 - lambda_init))
        o_ref[0] = jnp.concatenate(outs, axis=1).astype(o_ref.dtype)

    for n in range(t_all // tq):
        pl.when(i == n)(functools.partial(attend, n))


def _diff_attention(proj, qg2, kg2, sg, lq1, lk1, lq2, lk2, slopes, lambda_init, tq, heads):
    bsz, t, _ = proj.shape
    width = heads * B_V_DIM
    q_blk = A_COLS // width
    k_blk = (A_COLS + B_QK_COLS) // width
    v_blk = (A_COLS + 2 * B_QK_COLS) // width
    vec = lambda n: pl.BlockSpec((1, n), lambda b, h, i: (0, 0))
    return pl.pallas_call(
        functools.partial(_attn_kernel, tq=tq, heads=heads, lambda_init=lambda_init),
        grid=(bsz, B_HEADS // heads, t // tq),
        in_specs=[
            pl.BlockSpec((1, tq, width), lambda b, h, i: (b, i, q_blk + h)),
            pl.BlockSpec((1, t, width), lambda b, h, i: (b, 0, k_blk + h)),
            pl.BlockSpec((1, t, width), lambda b, h, i: (b, 0, v_blk + h)),
            vec(B_V_DIM), vec(B_V_DIM), vec(B_V_DIM),
            vec(B_QK_DIM), vec(B_QK_DIM), vec(B_QK_DIM), vec(B_QK_DIM),
            pl.BlockSpec((heads, 1, V7X_LANES), lambda b, h, i: (h, 0, 0)),
            pl.BlockSpec((heads, t, V7X_LANES), lambda b, h, i: (h, 0, 0)),
        ],
        out_specs=pl.BlockSpec((1, tq, width), lambda b, h, i: (b, i, h)),
        out_shape=jax.ShapeDtypeStruct((bsz, t, B_WIDTH), BF16),
        scratch_shapes=[pltpu.VMEM((heads, t, B_V_DIM + V7X_LANES), BF16),
                        pltpu.VMEM((heads, t, B_V_DIM), BF16),
                        pltpu.VMEM((heads, tq // 2, tq // 2), F32)],
        compiler_params=_params("arbitrary", "arbitrary", "arbitrary"),
        name="diff_attention",
    )(proj, proj, proj, qg2, kg2, sg, lq1, lk1, lq2, lk2, slopes, _alibi_key_bias(slopes, t))


def _mix_mlp_kernel(x_ref, ya_ref, yb_ref, yc_ref, wo_ref, g1_ref, g_ref, sc_ref, sh_ref, g2_ref,
                    w1_ref, w2_ref, o_ref, *, tf):
    y = jnp.dot(ya_ref[0], wo_ref[0:A_WIDTH, :], preferred_element_type=F32)
    y = y + jnp.dot(yb_ref[0], wo_ref[A_WIDTH:A_WIDTH + B_WIDTH, :], preferred_element_type=F32)
    y = y + jnp.dot(yc_ref[0], wo_ref[A_WIDTH + B_WIDTH:, :], preferred_element_type=F32)
    xn = x_ref[0] + g1_ref[0] * y
    h = _norm_mod(xn, g_ref[...], sc_ref[0], sh_ref[0]).astype(BF16)
    acc = None
    for f in range(w1_ref.shape[1] // tf):
        a = jnp.dot(h, w1_ref[:, f * tf:(f + 1) * tf], preferred_element_type=F32)
        a = jnp.maximum(a, 0.0)
        part = jnp.dot((a * a).astype(BF16), w2_ref[f * tf:(f + 1) * tf, :],
                       preferred_element_type=F32)
        acc = part if acc is None else acc + part
    o_ref[0] = xn + g2_ref[0] * acc


def _mix_mlp(x, ya, yb, yc, wo, g1, g, sc, sh, g2, w1, w2, layer, tm, tf):
    bsz, t, d = x.shape
    dff = w1.shape[2]
    row = pl.BlockSpec((1, 1, d), lambda b, i: (b, 0, 0))
    blk = lambda n: pl.BlockSpec((1, tm, n), lambda b, i: (b, i, 0))
    resident = lambda m, n: pl.BlockSpec((None, m, n), lambda b, i: (layer, 0, 0),
                                         pipeline_mode=pl.Buffered(1))
    return pl.pallas_call(
        functools.partial(_mix_mlp_kernel, tf=tf),
        grid=(bsz, t // tm),
        in_specs=[
            blk(d), blk(A_WIDTH), blk(B_WIDTH), blk(C_WIDTH),
            resident(d, d),
            row,
            pl.BlockSpec((1, d), lambda b, i: (0, 0)),
            row, row, row,
            resident(d, dff),
            resident(dff, d),
        ],
        out_specs=blk(d),
        out_shape=jax.ShapeDtypeStruct((bsz, t, d), F32),
        compiler_params=_params("arbitrary", "arbitrary"),
        name="mix_mlp",
    )(x, ya, yb, yc, wo, g1, g, sc, sh, g2, w1, w2)


def _alibi_slopes():
    s = jnp.asarray([2.0 ** (-8.0 * (h + 1) / B_HEADS) for h in range(B_HEADS)], F32)
    return jnp.broadcast_to(s[:, None, None], (B_HEADS, 1, V7X_LANES))


def _alibi_key_bias(slopes, t):
    rem = jnp.arange(t, dtype=F32)[None, :, None] * (slopes[:, :, 0:1] * LOG2E)
    pieces = []
    for _ in range(ALIBI_PIECES):
        bits = lax.bitcast_convert_type(rem, jnp.uint32) & jnp.uint32(0xFFFF0000)
        part = lax.bitcast_convert_type(bits, F32)
        pieces.append(part.astype(BF16))
        rem = rem - part
    table = jnp.concatenate(pieces, axis=-1)
    return jnp.pad(table, ((0, 0), (0, 0), (0, V7X_LANES - ALIBI_PIECES)))


def _pad_rows(w, before, after):
    return jnp.pad(w, ((before, after), (0, 0)))


def _layer(x, mod, p, big, layer, lambda_init, tiles):
    bsz = x.shape[0]
    sh1, sc1, g1, sh2, sc2, g2 = [m.reshape(bsz, 1, D_MODEL) for m in jnp.split(mod, 6, axis=-1)]
    row = lambda a: a.reshape(1, -1)

    proj, y_c = _in_projection_conv(
        x, row(p["norm1_g"]), sc1, sh1, big["w_in"], layer, p["conv_w"], row(p["conv_b"]),
        row(p["conv_ln_w"]), row(p["conv_ln_b"]), tiles["proj"])
    y_a = _rwkv_mix(
        proj, row(p["tshift_mu"]), row(p["decay_w0"]),
        _pad_rows(p["decay_w2"], 0, ICLR_LORA).astype(BF16), row(p["iclr_a0"]),
        _pad_rows(p["iclr_a2"], DECAY_LORA, 0).astype(BF16), p["gate_g2"].astype(BF16),
        row(p["k_k"]), row(p["k_a"]), row(p["r_k"]), row(p["lnx_w"]), row(p["lnx_b"]),
        tiles["rwkv"], RWKV_ROWS)
    two = lambda a: row(jnp.concatenate([a, a]))
    y_b = _diff_attention(
        proj, two(p["q_norm_g"]), two(p["k_norm_g"]), row(p["subln_g"]),
        row(p["lambda_q1"]), row(p["lambda_k1"]), row(p["lambda_q2"]), row(p["lambda_k2"]),
        _alibi_slopes(), lambda_init, tiles["attn"], ATTN_HEADS)
    return _mix_mlp(x, y_a, y_b, y_c, big["w_out"], g1, row(p["norm2_g"]), sc2, sh2, g2,
                    big["mlp_w1"], big["mlp_w2"], layer, tiles["mlp_tm"], tiles["mlp_tf"])


_LAYER_PARAMS = ("norm1_g", "w_in", "tshift_mu", "decay_w0", "decay_w2", "iclr_a0", "iclr_a2",
                 "gate_g2", "k_k", "k_a", "r_k", "lnx_w", "lnx_b", "q_norm_g", "k_norm_g",
                 "lambda_q1", "lambda_k1", "lambda_q2", "lambda_k2", "subln_g", "conv_w",
                 "conv_b", "conv_ln_w", "conv_ln_b", "w_out", "norm2_g", "mlp_w1", "mlp_w2")


_STACKED_MATRICES = ("w_in", "w_out", "mlp_w1", "mlp_w2")


def _block(x, c, ada_w, ada_b, params, tiles):
    mod = _ada_modulation(c, ada_w, ada_b)
    big = {k: params[k].astype(BF16) for k in _STACKED_MATRICES}
    for i in range(ada_w.shape[0]):
        lambda_init = 0.8 - 0.6 * math.exp(-0.3 * i)
        small = {k: v[i] for k, v in params.items() if k not in _STACKED_MATRICES}
        x = _layer(x, mod[i], small, big, i, lambda_init, tiles)
    return x


def kernel(x, c, ada_w, ada_b, norm1_g, w_in, tshift_mu, decay_w0, decay_w2, iclr_a0, iclr_a2,
           gate_g2, k_k, k_a, r_k, lnx_w, lnx_b, q_norm_g, k_norm_g, lambda_q1, lambda_k1,
           lambda_q2, lambda_k2, subln_g, conv_w, conv_b, conv_ln_w, conv_ln_b, w_out, norm2_g,
           mlp_w1, mlp_w2):
    values = (norm1_g, w_in, tshift_mu, decay_w0, decay_w2, iclr_a0, iclr_a2, gate_g2, k_k, k_a,
              r_k.reshape(r_k.shape[0], -1), lnx_w, lnx_b, q_norm_g, k_norm_g, lambda_q1,
              lambda_k1, lambda_q2, lambda_k2, subln_g, conv_w, conv_b, conv_ln_w, conv_ln_b,
              w_out, norm2_g, mlp_w1, mlp_w2)
    t = x.shape[1]
    tiles = dict(proj=min(PROJ_TM, t), rwkv=min(RWKV_TILE, t), attn=min(ATTN_TQ, t),
                 mlp_tm=min(MLP_TM, t), mlp_tf=MLP_TF)
    return _block(x, c, ada_w, ada_b, dict(zip(_LAYER_PARAMS, values)), tiles)
```

```python
import functools
import math

import jax
import jax.numpy as jnp
from jax import lax
from jax.experimental import pallas as pl
from jax.experimental.pallas import tpu as pltpu

F32 = jnp.float32
BF16 = jnp.bfloat16

D_MODEL = 1024
DEPTH = 2
CHUNK = 64
A_HEADS = 4
A_HEAD_DIM = 64
A_WIDTH = A_HEADS * A_HEAD_DIM
DECAY_LORA = 64
ICLR_LORA = 64
GATE_LORA = 128
A_COLS = 3 * A_WIDTH + DECAY_LORA + ICLR_LORA + GATE_LORA
B_HEADS = 4
B_QK_DIM = 64
B_V_DIM = 2 * B_QK_DIM
B_WIDTH = B_HEADS * B_V_DIM
B_QK_COLS = B_HEADS * 2 * B_QK_DIM
B_COLS = 2 * B_QK_COLS + B_WIDTH
C_WIDTH = D_MODEL - A_WIDTH - B_WIDTH
CONV_WIDTH = 31
C_COLS = 2 * C_WIDTH
N_IN = A_COLS + B_COLS + C_COLS
D_FF = 4 * D_MODEL
RMS_EPS = 1e-6
LN_EPS = 1e-5
LNX_EPS = 64e-5
NEG_INF = -1e30
EXP_NEG_HALF = math.exp(-0.5)

V7X_LANES = 128
V7X_SUBLANES = 8
V7X_VMEM_LIMIT_BYTES = 56 * 1024 * 1024

RWKV_CHUNK = 64
RWKV_TILE = 256
RWKV_ROWS = 4
ATTN_TQ = 512
ATTN_HEADS = 4
CONV_HALO = 32
PROJ_TM = 1024
MLP_TM = 1024
MLP_TF = 1024


def _params(*sem):
    return pltpu.CompilerParams(dimension_semantics=sem,
                                vmem_limit_bytes=V7X_VMEM_LIMIT_BYTES)


def _dot(a, b):
    return jnp.dot(a.astype(BF16), b.astype(BF16), preferred_element_type=F32)


def _dot_nt(a, b):
    return lax.dot_general(a.astype(BF16), b.astype(BF16), (((1,), (1,)), ((), ())),
                           preferred_element_type=F32)


def _dot_tn(a, b):
    return lax.dot_general(a.astype(BF16), b.astype(BF16), (((0,), (0,)), ((), ())),
                           preferred_element_type=F32)


def _split_dot(a, b_exact, parts):
    out = None
    rem = a
    for _ in range(parts):
        piece = rem.astype(BF16)
        term = jnp.dot(piece, b_exact, preferred_element_type=F32)
        out = term if out is None else out + term
        rem = rem - piece.astype(F32)
    return out


def _sigmoid(x):
    return 1.0 / (1.0 + jnp.exp(-x))


def _ada_kernel(c_ref, w_ref, b_ref, o_ref):
    c = c_ref[...]
    cond = c * _sigmoid(c)
    o_ref[0] = _dot(cond, w_ref[0]) + b_ref[0]


def _ada_modulation(c, ada_w, ada_b):
    depth, d, n = ada_w.shape
    bsz = c.shape[0]
    tn = 1536
    return pl.pallas_call(
        _ada_kernel,
        grid=(depth, n // tn),
        in_specs=[
            pl.BlockSpec((bsz, d), lambda l, j: (0, 0)),
            pl.BlockSpec((1, d, tn), lambda l, j: (l, 0, j)),
            pl.BlockSpec((1, 1, tn), lambda l, j: (l, 0, j)),
        ],
        out_specs=pl.BlockSpec((1, bsz, tn), lambda l, j: (l, 0, j)),
        out_shape=jax.ShapeDtypeStruct((depth, bsz, n), F32),
        compiler_params=_params("arbitrary", "arbitrary"),
        name="ada_modulation",
    )(c, ada_w, ada_b.reshape(depth, 1, n))


def _norm_mod(x, g, sc, sh):
    ms = jnp.mean(x * x, axis=-1, keepdims=True)
    y = x * lax.rsqrt(ms + RMS_EPS) * g
    return y * (1.0 + sc) + sh


def _conv_module(pc, w_ref, b_ref, lw_ref, lb_ref, o_ref, hbuf_ref, tile, rows):
    @pl.when(pl.program_id(1) == 0)
    def _():
        hbuf_ref[0, 0:CONV_HALO, :] = jnp.zeros((CONV_HALO, C_WIDTH), F32)

    n = CONV_HALO + tile
    hbuf_ref[0, CONV_HALO:n, :] = pc[:, :C_WIDTH] * _sigmoid(pc[:, C_WIDTH:])
    full = hbuf_ref[0]
    for s in range(1, V7X_SUBLANES):
        hbuf_ref[s] = pltpu.roll(full, n - s, 0)
    lead = CONV_HALO - (CONV_WIDTH - 1)
    for c in range(tile // rows):
        acc = jnp.zeros((rows, C_WIDTH), F32)
        for j in range(CONV_WIDTH):
            start = c * rows + lead + j
            s = start % V7X_SUBLANES
            acc = acc + w_ref[j:j + 1, :] * hbuf_ref[s, start - s:start - s + rows, :]
        acc = acc + b_ref[...]
        mu = jnp.mean(acc, axis=-1, keepdims=True)
        d = acc - mu
        var = jnp.mean(d * d, axis=-1, keepdims=True)
        y = d * lax.rsqrt(var + LN_EPS) * lw_ref[...] + lb_ref[...]
        o_ref[0, c * rows:(c + 1) * rows, :] = (y * _sigmoid(y)).astype(o_ref.dtype)
    hbuf_ref[0, 0:CONV_HALO, :] = hbuf_ref[0, tile:n, :]


def _inproj_kernel(x_ref, g_ref, sc_ref, sh_ref, w_ref, cw_ref, cb_ref, clw_ref, clb_ref,
                   o_ref, yc_ref, hbuf_ref, *, tn, rows):
    tile = x_ref.shape[1]
    h = _norm_mod(x_ref[0], g_ref[...], sc_ref[0], sh_ref[0]).astype(BF16)
    c0 = A_COLS + B_COLS
    pc = jnp.dot(h, w_ref[:, c0:], preferred_element_type=F32)
    o_ref[0, :, c0:] = pc.astype(o_ref.dtype)
    _conv_module(pc, cw_ref, cb_ref, clw_ref, clb_ref, yc_ref, hbuf_ref, tile, rows)
    for j in range(c0 // tn):
        o_ref[0, :, j * tn:(j + 1) * tn] = jnp.dot(
            h, w_ref[:, j * tn:(j + 1) * tn], preferred_element_type=F32).astype(o_ref.dtype)


def _in_projection_conv(x, g, sc, sh, w, layer, conv_w, conv_b, ln_w, ln_b, tm):
    bsz, t, d = x.shape
    n = w.shape[2]
    vec = lambda k: pl.BlockSpec((1, k), lambda b, i: (0, 0))
    row = pl.BlockSpec((1, 1, d), lambda b, i: (b, 0, 0))
    return pl.pallas_call(
        functools.partial(_inproj_kernel, tn=C_COLS, rows=64),
        grid=(bsz, t // tm),
        in_specs=[
            pl.BlockSpec((1, tm, d), lambda b, i: (b, i, 0)),
            vec(d), row, row,
            pl.BlockSpec((None, d, n), lambda b, i: (layer, 0, 0), pipeline_mode=pl.Buffered(1)),
            pl.BlockSpec((CONV_WIDTH, C_WIDTH), lambda b, i: (0, 0)),
            vec(C_WIDTH), vec(C_WIDTH), vec(C_WIDTH),
        ],
        out_specs=[pl.BlockSpec((1, tm, n), lambda b, i: (b, i, 0)),
                   pl.BlockSpec((1, tm, C_WIDTH), lambda b, i: (b, i, 0))],
        out_shape=[jax.ShapeDtypeStruct((bsz, t, n), BF16),
                   jax.ShapeDtypeStruct((bsz, t, C_WIDTH), BF16)],
        scratch_shapes=[pltpu.VMEM((V7X_SUBLANES, CONV_HALO + tm, C_WIDTH), F32)],
        compiler_params=_params("arbitrary", "arbitrary"),
        name="in_projection_conv",
    )(x, g, sc, sh, w, conv_w, conv_b, ln_w, ln_b)


def _rwkv_kernel(pa_ref, mu_ref, w0_ref, w2_ref, a0_ref, a2_ref, g2_ref, kk_ref, ka_ref,
                 rk_ref, lw_ref, lb_ref, ho_ref, tri_ref, o_ref, carry_ref, s_ref, *, rows, tile,
                 chunk):
    aw = A_WIDTH
    hd = A_HEAD_DIM
    n_chunks = tile // chunk

    @pl.when(pl.program_id(1) == 0)
    def _():
        carry_ref[...] = jnp.zeros_like(carry_ref)
        s_ref[...] = jnp.zeros_like(s_ref)

    ri = lax.broadcasted_iota(jnp.int32, (aw, aw), 0)
    ci = lax.broadcasted_iota(jnp.int32, (aw, aw), 1)
    same_head = (ri // hd) == (ci // hd)
    head_ones = ho_ref[...]
    eye_full = ri == ci
    lane_head = lax.broadcasted_iota(jnp.int32, (hd, aw), 1) // hd
    t_c = lax.broadcasted_iota(jnp.int32, (chunk, aw), 0)
    s_c = lax.broadcasted_iota(jnp.int32, (chunk, aw), 1) % chunk
    strict = s_c < t_c
    incl = s_c <= t_c
    eye_c = (s_c == t_c).astype(F32)
    tri = tri_ref[...]
    row = lax.broadcasted_iota(jnp.int32, (tile, 1), 0)

    def head_sum(x, parts=1):
        return _split_dot(x, head_ones, parts)

    def blockdiag(x):
        return jnp.where(same_head, jnp.concatenate([x] * A_HEADS, axis=0), 0.0)

    seq = []
    for bi in range(rows):
        pa = pa_ref[bi].astype(F32)
        crow = bi * V7X_SUBLANES
        prev = jnp.where(row == 0, carry_ref[crow:crow + 1, :], pltpu.roll(pa, 1, 0))
        carry_ref[crow:crow + 1, :] = pa[tile - 1:tile, :]
        xs = pa + mu_ref[...] * (prev - pa)
        r = xs[:, 0:aw]
        k = xs[:, aw:2 * aw]
        v = xs[:, 2 * aw:3 * aw]
        lora = xs[:, 3 * aw:3 * aw + DECAY_LORA + ICLR_LORA]
        dg = xs[:, 3 * aw + DECAY_LORA + ICLR_LORA:]
        u = w0_ref[...] + _dot(jnp.tanh(lora), w2_ref[...])
        logdecay = -EXP_NEG_HALF * _sigmoid(u)
        iclr = _sigmoid(a0_ref[...] + _dot(lora, a2_ref[...]))
        gate = _dot(_sigmoid(dg), g2_ref[...])
        kk = k * kk_ref[...]
        kk = kk * lax.rsqrt(jnp.maximum(head_sum(kk * kk, 2), 1e-24))
        k = k * (1.0 + (iclr - 1.0) * ka_ref[...])
        b = kk * iclr
        cum = None
        rem = logdecay
        for _ in range(3):
            piece = rem.astype(BF16)
            term = jnp.dot(tri, piece, preferred_element_type=F32)
            cum = term if cum is None else cum + term
            rem = rem - piece.astype(F32)
        seq.append(dict(r=r, k=k, v=v, kk=kk, b=b, ld=logdecay, cum=cum, gate=gate))

    chains = [(bi, c) for bi in range(rows) for c in range(n_chunks)]
    ch = []
    for bi, c in chains:
        q = seq[bi]
        sl = slice(c * chunk, (c + 1) * chunk)
        cum_c = q["cum"][sl]
        total = cum_c[chunk - 1:chunk, :]
        w_inv = jnp.exp(-cum_c)
        w_end = jnp.exp(total - cum_c)
        ch.append(dict(
            rt=q["r"][sl] * jnp.exp(cum_c), at=-q["kk"][sl] * jnp.exp(cum_c - q["ld"][sl]),
            bt=q["b"][sl] * w_inv, kt=q["k"][sl] * w_inv, be=q["b"][sl] * w_end,
            ke=q["k"][sl] * w_end, vc=q["v"][sl], total=total))
    for d in ch:
        lhs = jnp.concatenate([d["at"], d["rt"]], axis=0)
        rhs = jnp.concatenate([blockdiag(d["bt"]), blockdiag(d["kt"])], axis=0)
        prod = _dot_nt(lhs, rhs)
        d["low"] = jnp.where(strict, prod[:chunk, :aw], 0.0)
        d["a_ak"] = jnp.where(strict, prod[:chunk, aw:], 0.0)
        d["a_rb"] = jnp.where(incl, prod[chunk:, :aw], 0.0)
        d["a_rk"] = jnp.where(incl, prod[chunk:, aw:], 0.0)
    for d in ch:
        d["pw"] = _dot(d["low"], blockdiag(d["low"]))
        d["ti"] = eye_c + d["low"]
        d["akv"] = _dot(d["a_ak"], blockdiag(d["vc"]))
        d["y_in"] = _dot(d["a_rk"], blockdiag(d["vc"]))
    for _ in range(int(math.log2(chunk)) - 2):
        for d in ch:
            both = _dot(jnp.concatenate([d["pw"], d["ti"]], axis=0), blockdiag(d["pw"]))
            d["pw"] = both[:chunk]
            d["ti"] = d["ti"] + both[chunk:]
    for d in ch:
        d["ti"] = d["ti"] + _dot(d["ti"], blockdiag(d["pw"]))
    for d in ch:
        both = _dot(d["ti"], jnp.concatenate([blockdiag(d["at"]), blockdiag(d["akv"])], axis=1))
        d["ah"] = both[:, :aw]
        d["vh"] = both[:, aw:]
    for d in ch:
        both = _dot(d["a_rb"], jnp.concatenate([blockdiag(d["ah"]), blockdiag(d["vh"])], axis=1))
        d["rh"] = d["rt"] + both[:, :aw]
        d["y_in"] = d["y_in"] + both[:, aw:]
        trans = jnp.where(same_head, _dot_tn(d["ah"], d["be"]), 0.0)
        d["trans"] = trans + jnp.where(eye_full, jnp.exp(d["total"]), 0.0)
        add_full = _dot_tn(jnp.concatenate([d["vh"], d["vc"]], axis=0),
                           jnp.concatenate([d["be"], d["ke"]], axis=0))
        add = None
        for h in range(A_HEADS):
            blk = jnp.where(lane_head == h, add_full[h * hd:(h + 1) * hd], 0.0)
            add = blk if add is None else add + blk
        d["add"] = add

    states = [s_ref[bi * hd:(bi + 1) * hd, :] for bi in range(rows)]
    ys = [[] for _ in range(rows)]
    for c in range(n_chunks):
        for bi in range(rows):
            d = ch[bi * n_chunks + c]
            ys[bi].append(_dot_nt(d["rh"], blockdiag(states[bi])) + d["y_in"])
            states[bi] = _dot(states[bi], d["trans"]) + d["add"]

    inv_n = 1.0 / hd
    for bi in range(rows):
        q = seq[bi]
        s_ref[bi * hd:(bi + 1) * hd, :] = states[bi]
        y = jnp.concatenate(ys[bi], axis=0)
        mu_h = head_sum(y) * inv_n
        yc = y - mu_h
        var_h = head_sum(yc * yc) * inv_n
        yn = yc * lax.rsqrt(var_h + LNX_EPS) * lw_ref[...] + lb_ref[...]
        bonus = head_sum(q["r"] * q["k"] * rk_ref[...]) * q["v"]
        o_ref[bi] = ((yn + bonus) * q["gate"]).astype(o_ref.dtype)


def _rwkv_mix(proj, mu, w0, w2p, a0, a2p, g2, k_k, k_a, r_k, lnx_w, lnx_b, tile, rows):
    bsz, t, _ = proj.shape
    rows = math.gcd(rows, bsz)
    vec = lambda n: pl.BlockSpec((1, n), lambda b, i: (0, 0))
    mat = lambda m, n: pl.BlockSpec((m, n), lambda b, i: (0, 0))
    lane_head = jnp.arange(A_WIDTH) // A_HEAD_DIM
    head_ones = (lane_head[:, None] == lane_head[None, :]).astype(BF16)
    pos = jnp.arange(tile)
    tri = ((pos[None, :] <= pos[:, None])
           & (pos[None, :] // RWKV_CHUNK == pos[:, None] // RWKV_CHUNK)).astype(BF16)
    return pl.pallas_call(
        functools.partial(_rwkv_kernel, rows=rows, tile=tile, chunk=RWKV_CHUNK),
        grid=(bsz // rows, t // tile),
        in_specs=[
            pl.BlockSpec((rows, tile, A_COLS), lambda b, i: (b, i, 0)),
            vec(A_COLS), vec(A_WIDTH), mat(DECAY_LORA + ICLR_LORA, A_WIDTH),
            vec(A_WIDTH), mat(DECAY_LORA + ICLR_LORA, A_WIDTH), mat(GATE_LORA, A_WIDTH),
            vec(A_WIDTH), vec(A_WIDTH), vec(A_WIDTH), vec(A_WIDTH), vec(A_WIDTH),
            mat(A_WIDTH, A_WIDTH), mat(tile, tile),
        ],
        out_specs=pl.BlockSpec((rows, tile, A_WIDTH), lambda b, i: (b, i, 0)),
        out_shape=jax.ShapeDtypeStruct((bsz, t, A_WIDTH), BF16),
        scratch_shapes=[pltpu.VMEM((rows * V7X_SUBLANES, A_COLS), F32),
                        pltpu.VMEM((rows * A_HEAD_DIM, A_WIDTH), F32)],
        compiler_params=_params("arbitrary", "arbitrary"),
        name="rwkv7_mix",
    )(proj, mu, w0, w2p, a0, a2p, g2, k_k, k_a, r_k, lnx_w, lnx_b, head_ones, tri)


LOG2E = math.log2(math.e)
ALIBI_PIECES = 3


def _half_rms(x, gain, scale):
    ri = lax.broadcasted_iota(jnp.int32, (x.shape[1], x.shape[1]), 0)
    ci = lax.broadcasted_iota(jnp.int32, (x.shape[1], x.shape[1]), 1)
    half_ones = ((ri // B_QK_DIM) == (ci // B_QK_DIM)).astype(F32).astype(BF16)
    ms = _split_dot(x * x, half_ones, 1) * (1.0 / B_QK_DIM)
    return x * lax.rsqrt(ms + RMS_EPS) * (gain * scale)


def _lane_fold(x, op):
    parts = [x[:, k * V7X_LANES:(k + 1) * V7X_LANES] for k in range(x.shape[1] // V7X_LANES)]
    while len(parts) > 1:
        pairs = [op(parts[a], parts[a + 1]) for a in range(0, len(parts) - 1, 2)]
        parts = pairs + parts[len(parts) - len(parts) % 2:]
    return parts[0]


def _attn_kernel(q_ref, k_ref, v_ref, qg_ref, kg_ref, sg_ref, lq1_ref, lk1_ref, lq2_ref,
                 lk2_ref, slope_ref, bias_ref, o_ref, kx_ref, vb_ref, corr_ref, *, tq, heads,
                 lambda_init):
    i = pl.program_id(2)
    t_all = k_ref.shape[1]
    half = tq // 2
    hw = B_V_DIM
    hsl = [slice(h * hw, (h + 1) * hw) for h in range(heads)]

    @pl.when(i == 0)
    def _():
        r = lax.broadcasted_iota(jnp.int32, (half, half), 0)
        c = lax.broadcasted_iota(jnp.int32, (half, half), 1)
        for h in range(heads):
            slope2 = slope_ref[h][:, 0:1] * LOG2E
            kx_ref[h, :, 0:hw] = _half_rms(k_ref[0, :, hsl[h]].astype(F32), kg_ref[...],
                                           1.0).astype(BF16)
            kx_ref[h, :, hw:] = bias_ref[h]
            vb_ref[h] = v_ref[0, :, hsl[h]].astype(BF16)
            after = jnp.where(c > r, (c - r).astype(F32) * (-2.0 * slope2), 0.0)
            corr_ref[h] = jnp.where((c // CHUNK) <= (r // CHUNK), after, NEG_INF)

    lane = lax.broadcasted_iota(jnp.int32, (tq, V7X_LANES), 1)
    lo = lane < B_QK_DIM
    ones = (lane < ALIBI_PIECES).astype(F32)
    qxs = []
    for h in range(heads):
        qn = _half_rms(q_ref[0, :, hsl[h]].astype(F32), qg_ref[...],
                       LOG2E * B_QK_DIM ** -0.5)
        q0 = jnp.concatenate([jnp.where(lo, qn, 0.0), ones], axis=1)
        q1 = jnp.concatenate([jnp.where(lo, 0.0, qn), ones], axis=1)
        qxs.append(jnp.concatenate([q0[:half], q1[:half], q0[half:], q1[half:]],
                                   axis=0).astype(BF16))

    lam = (jnp.exp(jnp.sum(lq1_ref[...] * lk1_ref[...], axis=-1, keepdims=True))
           - jnp.exp(jnp.sum(lq2_ref[...] * lk2_ref[...], axis=-1, keepdims=True))
           + lambda_init)

    def scores(h, lhs, start, size):
        return lax.dot_general(lhs, kx_ref[h, start:start + size, :], (((1,), (1,)), ((), ())),
                               preferred_element_type=F32)

    def update(h, s, carry, start, size):
        m, l, acc = carry
        m_new = jnp.maximum(m, jnp.max(s, axis=-1, keepdims=True))
        alpha = jnp.exp2(m - m_new)
        p = jnp.exp2(s - m_new)
        l = alpha * l + jnp.sum(p, axis=-1, keepdims=True)
        pv = jnp.dot(p.astype(BF16), vb_ref[h, start:start + size, :], preferred_element_type=F32)
        return m_new, l, alpha * acc + pv

    def attend(n):
        d0 = n * tq
        carry = [(jnp.full((2 * tq, 1), NEG_INF, F32), jnp.zeros((2 * tq, 1), F32),
                  jnp.zeros((2 * tq, B_V_DIM), F32)) for _ in range(heads)]
        for j in range(n):
            for h in range(heads):
                carry[h] = update(h, scores(h, qxs[h], j * tq, tq), carry[h], j * tq, tq)
        corr2 = [jnp.concatenate([corr_ref[h], corr_ref[h]], axis=0) for h in range(heads)]
        for h in range(heads):
            s1 = scores(h, qxs[h], d0, half)
            s1 = jnp.concatenate([s1[:tq] + corr2[h], s1[tq:]], axis=0)
            carry[h] = update(h, s1, carry[h], d0, half)
        outs = []
        for h in range(heads):
            m, l, acc = carry[h]
            s2 = scores(h, qxs[h][tq:], d0 + half, half) + corr2[h]
            _, l_b, acc_b = update(h, s2, (m[tq:], l[tq:], acc[tq:]), d0 + half, half)
            o_a = acc[:tq] / l[:tq]
            o_b = acc_b / l_b
            d = jnp.concatenate([o_a[:half] - lam * o_a[half:], o_b[:half] - lam * o_b[half:]],
                                axis=0)
            ms = jnp.mean(d * d, axis=-1, keepdims=True)
            outs.append(d * lax.rsqrt(ms + RMS_EPS) * sg_ref[...] * (1.0 - lambda_init))
        o_ref[0] = jnp.concatenate(outs, axis=1).astype(o_ref.dtype)

    for n in range(t_all // tq):
        pl.when(i == n)(functools.partial(attend, n))


def _diff_attention(proj, qg2, kg2, sg, lq1, lk1, lq2, lk2, slopes, lambda_init, tq, heads):
    bsz, t, _ = proj.shape
    width = heads * B_V_DIM
    q_blk = A_COLS // width
    k_blk = (A_COLS + B_QK_COLS) // width
    v_blk = (A_COLS + 2 * B_QK_COLS) // width
    vec = lambda n: pl.BlockSpec((1, n), lambda b, h, i: (0, 0))
    return pl.pallas_call(
        functools.partial(_attn_kernel, tq=tq, heads=heads, lambda_init=lambda_init),
        grid=(bsz, B_HEADS // heads, t // tq),
        in_specs=[
            pl.BlockSpec((1, tq, width), lambda b, h, i: (b, i, q_blk + h)),
            pl.BlockSpec((1, t, width), lambda b, h, i: (b, 0, k_blk + h)),
            pl.BlockSpec((1, t, width), lambda b, h, i: (b, 0, v_blk + h)),
            vec(B_V_DIM), vec(B_V_DIM), vec(B_V_DIM),
            vec(B_QK_DIM), vec(B_QK_DIM), vec(B_QK_DIM), vec(B_QK_DIM),
            pl.BlockSpec((heads, 1, V7X_LANES), lambda b, h, i: (h, 0, 0)),
            pl.BlockSpec((heads, t, V7X_LANES), lambda b, h, i: (h, 0, 0)),
        ],
        out_specs=pl.BlockSpec((1, tq, width), lambda b, h, i: (b, i, h)),
        out_shape=jax.ShapeDtypeStruct((bsz, t, B_WIDTH), BF16),
        scratch_shapes=[pltpu.VMEM((heads, t, B_V_DIM + V7X_LANES), BF16),
                        pltpu.VMEM((heads, t, B_V_DIM), BF16),
                        pltpu.VMEM((heads, tq // 2, tq // 2), F32)],
        compiler_params=_params("arbitrary", "arbitrary", "arbitrary"),
        name="diff_attention",
    )(proj, proj, proj, qg2, kg2, sg, lq1, lk1, lq2, lk2, slopes, _alibi_key_bias(slopes, t))


def _mix_mlp_kernel(x_ref, ya_ref, yb_ref, yc_ref, wo_ref, g1_ref, g_ref, sc_ref, sh_ref, g2_ref,
                    w1_ref, w2_ref, o_ref, *, tf):
    y = jnp.dot(ya_ref[0], wo_ref[0:A_WIDTH, :], preferred_element_type=F32)
    y = y + jnp.dot(yb_ref[0], wo_ref[A_WIDTH:A_WIDTH + B_WIDTH, :], preferred_element_type=F32)
    y = y + jnp.dot(yc_ref[0], wo_ref[A_WIDTH + B_WIDTH:, :], preferred_element_type=F32)
    xn = x_ref[0] + g1_ref[0] * y
    h = _norm_mod(xn, g_ref[...], sc_ref[0], sh_ref[0]).astype(BF16)
    acc = None
    for f in range(w1_ref.shape[1] // tf):
        a = jnp.dot(h, w1_ref[:, f * tf:(f + 1) * tf], preferred_element_type=F32)
        a = jnp.maximum(a, 0.0)
        part = jnp.dot((a * a).astype(BF16), w2_ref[f * tf:(f + 1) * tf, :],
                       preferred_element_type=F32)
        acc = part if acc is None else acc + part
    o_ref[0] = xn + g2_ref[0] * acc


def _mix_mlp(x, ya, yb, yc, wo, g1, g, sc, sh, g2, w1, w2, layer, tm, tf):
    bsz, t, d = x.shape
    dff = w1.shape[2]
    row = pl.BlockSpec((1, 1, d), lambda b, i: (b, 0, 0))
    blk = lambda n: pl.BlockSpec((1, tm, n), lambda b, i: (b, i, 0))
    resident = lambda m, n: pl.BlockSpec((None, m, n), lambda b, i: (layer, 0, 0),
                                         pipeline_mode=pl.Buffered(1))
    return pl.pallas_call(
        functools.partial(_mix_mlp_kernel, tf=tf),
        grid=(bsz, t // tm),
        in_specs=[
            blk(d), blk(A_WIDTH), blk(B_WIDTH), blk(C_WIDTH),
            resident(d, d),
            row,
            pl.BlockSpec((1, d), lambda b, i: (0, 0)),
            row, row, row,
            resident(d, dff),
            resident(dff, d),
        ],
        out_specs=blk(d),
        out_shape=jax.ShapeDtypeStruct((bsz, t, d), F32),
        compiler_params=_params("arbitrary", "arbitrary"),
        name="mix_mlp",
    )(x, ya, yb, yc, wo, g1, g, sc, sh, g2, w1, w2)


def _alibi_slopes():
    s = jnp.asarray([2.0 ** (-8.0 * (h + 1) / B_HEADS) for h in range(B_HEADS)], F32)
    return jnp.broadcast_to(s[:, None, None], (B_HEADS, 1, V7X_LANES))


def _alibi_key_bias(slopes, t):
    rem = jnp.arange(t, dtype=F32)[None, :, None] * (slopes[:, :, 0:1] * LOG2E)
    pieces = []
    for _ in range(ALIBI_PIECES):
        bits = lax.bitcast_convert_type(rem, jnp.uint32) & jnp.uint32(0xFFFF0000)
        part = lax.bitcast_convert_type(bits, F32)
        pieces.append(part.astype(BF16))
        rem = rem - part
    table = jnp.concatenate(pieces, axis=-1)
    return jnp.pad(table, ((0, 0), (0, 0), (0, V7X_LANES - ALIBI_PIECES)))


def _pad_rows(w, before, after):
    return jnp.pad(w, ((before, after), (0, 0)))


def _layer(x, mod, p, big, layer, lambda_init, tiles):
    bsz = x.shape[0]
    sh1, sc1, g1, sh2, sc2, g2 = [m.reshape(bsz, 1, D_MODEL) for m in jnp.split(mod, 6, axis=-1)]
    row = lambda a: a.reshape(1, -1)

    proj, y_c = _in_projection_conv(
        x, row(p["norm1_g"]), sc1, sh1, big["w_in"], layer, p["conv_w"], row(p["conv_b"]),
        row(p["conv_ln_w"]), row(p["conv_ln_b"]), tiles["proj"])
    y_a = _rwkv_mix(
        proj, row(p["tshift_mu"]), row(p["decay_w0"]),
        _pad_rows(p["decay_w2"], 0, ICLR_LORA).astype(BF16), row(p["iclr_a0"]),
        _pad_rows(p["iclr_a2"], DECAY_LORA, 0).astype(BF16), p["gate_g2"].astype(BF16),
        row(p["k_k"]), row(p["k_a"]), row(p["r_k"]), row(p["lnx_w"]), row(p["lnx_b"]),
        tiles["rwkv"], RWKV_ROWS)
    two = lambda a: row(jnp.concatenate([a, a]))
    y_b = _diff_attention(
        proj, two(p["q_norm_g"]), two(p["k_norm_g"]), row(p["subln_g"]),
        row(p["lambda_q1"]), row(p["lambda_k1"]), row(p["lambda_q2"]), row(p["lambda_k2"]),
        _alibi_slopes(), lambda_init, tiles["attn"], ATTN_HEADS)
    return _mix_mlp(x, y_a, y_b, y_c, big["w_out"], g1, row(p["norm2_g"]), sc2, sh2, g2,
                    big["mlp_w1"], big["mlp_w2"], layer, tiles["mlp_tm"], tiles["mlp_tf"])


_LAYER_PARAMS = ("norm1_g", "w_in", "tshift_mu", "decay_w0", "decay_w2", "iclr_a0", "iclr_a2",
                 "gate_g2", "k_k", "k_a", "r_k", "lnx_w", "lnx_b", "q_norm_g", "k_norm_g",
                 "lambda_q1", "lambda_k1", "lambda_q2", "lambda_k2", "subln_g", "conv_w",
                 "conv_b", "conv_ln_w", "conv_ln_b", "w_out", "norm2_g", "mlp_w1", "mlp_w2")


_STACKED_MATRICES = ("w_in", "w_out", "mlp_w1", "mlp_w2")


def _block(x, c, ada_w, ada_b, params, tiles):
    mod = _ada_modulation(c, ada_w, ada_b)
    big = {k: params[k].astype(BF16) for k in _STACKED_MATRICES}
    for i in range(ada_w.shape[0]):
        lambda_init = 0.8 - 0.6 * math.exp(-0.3 * i)
        small = {k: v[i] for k, v in params.items() if k not in _STACKED_MATRICES}
        x = _layer(x, mod[i], small, big, i, lambda_init, tiles)
    return x


def kernel(x, c, ada_w, ada_b, norm1_g, w_in, tshift_mu, decay_w0, decay_w2, iclr_a0, iclr_a2,
           gate_g2, k_k, k_a, r_k, lnx_w, lnx_b, q_norm_g, k_norm_g, lambda_q1, lambda_k1,
           lambda_q2, lambda_k2, subln_g, conv_w, conv_b, conv_ln_w, conv_ln_b, w_out, norm2_g,
           mlp_w1, mlp_w2):
    values = (norm1_g, w_in, tshift_mu, decay_w0, decay_w2, iclr_a0, iclr_a2, gate_g2, k_k, k_a,
              r_k.reshape(r_k.shape[0], -1), lnx_w, lnx_b, q_norm_g, k_norm_g, lambda_q1,
              lambda_k1, lambda_q2, lambda_k2, subln_g, conv_w, conv_b, conv_ln_w, conv_ln_b,
              w_out, norm2_g, mlp_w1, mlp_w2)
    t = x.shape[1]
    tiles = dict(proj=min(PROJ_TM, t), rwkv=min(RWKV_TILE, t), attn=min(ATTN_TQ, t),
                 mlp_tm=min(MLP_TM, t), mlp_tf=MLP_TF)
    return _block(x, c, ada_w, ada_b, dict(zip(_LAYER_PARAMS, values)), tiles)
```
